```python
import math
import jax, jax.numpy as jnp
from jax import lax
import numpy as np

D_MODEL = 1024
BATCH = 4
SEQ = 4096
DEPTH = 2

CHUNK = 64
N_META = 16
N_A_LAYERS = DEPTH // 2
N_B_LAYERS = DEPTH - N_A_LAYERS
DN_ALPHA = (2.0 * DEPTH) ** 0.25
DN_BETA = (8.0 * DEPTH) ** -0.25
LN_EPS = 1e-5
RMS_EPS = 1e-6

SSM_D_INNER = 2 * D_MODEL
SSM_HEAD_DIM = 64
SSM_HEADS = SSM_D_INNER // SSM_HEAD_DIM
SSM_GROUPS = 4
SSM_STATE = 128
SSM_CONV = 4
SSM_CONV_DIM = SSM_D_INNER + 2 * SSM_GROUPS * SSM_STATE
SSM_IN_DIM = SSM_D_INNER + SSM_CONV_DIM + SSM_HEADS
SSM_PAD = (-N_META) % CHUNK

MLA_HEADS = 16
Q_LORA = 384
KV_LORA = 256
QK_NOPE = 64
QK_ROPE = 32
V_HEAD = 64
ROPE_THETA = 10000.0
Q_BLOCK = 128
ATT_PAD = (-N_META) % Q_BLOCK

D_FF = 2816
FFN_CONV = 3

kernel_name = "yoco_mamba2_mla_convffn_deepnorm"


def layer_norm(x, g, b):
    xf = x.astype(jnp.float32)
    mu = jnp.mean(xf, axis=-1, keepdims=True)
    var = jnp.mean(jnp.square(xf - mu), axis=-1, keepdims=True)
    return ((xf - mu) * lax.rsqrt(var + LN_EPS) * g + b).astype(x.dtype)


def rms_norm(x, g):
    xf = x.astype(jnp.float32)
    y = xf * lax.rsqrt(jnp.mean(jnp.square(xf), axis=-1, keepdims=True) + RMS_EPS)
    return y.astype(x.dtype) * g


def causal_dwconv(x, w, b):
    k = w.shape[0]
    y = lax.conv_general_dilated(x, w[:, None, :].astype(x.dtype), window_strides=(1,),
                                 padding=[(k - 1, 0)],
                                 dimension_numbers=("NWC", "WIO", "NWC"),
                                 feature_group_count=x.shape[-1])
    return y + b


def rope_tables(length):
    half = QK_ROPE // 2
    pos = jnp.arange(length, dtype=jnp.float32)
    inv_freq = ROPE_THETA ** (-jnp.arange(half, dtype=jnp.float32) * 2.0 / QK_ROPE)
    ang = pos[:, None] * inv_freq[None, :]
    return jnp.cos(ang), jnp.sin(ang)


def rope(x, cos, sin):
    half = QK_ROPE // 2
    x1, x2 = x[..., :half], x[..., half:]
    return jnp.concatenate([x1 * cos - x2 * sin, x2 * cos + x1 * sin], axis=-1)


def ssd_scan(x, dA, bm, cm):
    b, t, h, p = x.shape
    g, n = bm.shape[2], bm.shape[3]
    r = h // g
    c = t // CHUNK
    xc = x.reshape(b, c, CHUNK, g, r, p)
    bc = bm.reshape(b, c, CHUNK, g, n)
    cc = cm.reshape(b, c, CHUNK, g, n)
    a = dA.astype(jnp.float32).reshape(b, c, CHUNK, g, r).transpose(0, 3, 4, 1, 2)
    a_cs = jnp.cumsum(a, axis=-1)
    tri = jnp.tril(jnp.ones((CHUNK, CHUNK), dtype=bool))
    seg = a_cs[..., :, None] - a_cs[..., None, :]
    decay_in = jnp.where(tri, jnp.exp(jnp.where(tri, seg, 0.0)), 0.0)
    cb = jnp.einsum("bclgn,bcsgn->bgcls", cc, bc)
    y_diag = jnp.einsum("bgrcls,bcsgrp->bclgrp", cb[:, :, None] * decay_in, xc)
    decay_states = jnp.exp(a_cs[..., -1:] - a_cs)
    states = jnp.einsum("bcsgn,bgrcs,bcsgrp->cbgrpn", bc, decay_states, xc)
    chunk_decay = jnp.exp(a_cs[..., -1]).transpose(3, 0, 1, 2)

    def step(state, inp):
        st, dc = inp
        return state * dc[..., None, None] + st, state

    h0 = jnp.zeros((b, g, r, p, n), dtype=states.dtype)
    _, prev = lax.scan(step, h0, (states, chunk_decay))
    y_off = jnp.einsum("bclgn,cbgrpn,bgrcl->bclgrp", cc, prev, jnp.exp(a_cs))
    return (y_diag + y_off).reshape(b, t, h, p)


def mamba2_mixer(u, w_in, conv_w, conv_b, dt_bias, a_log, d_skip, norm_g, w_out):
    bsz, length, _ = u.shape
    zxbcdt = u @ w_in
    z, xbc, dt = jnp.split(zxbcdt, [SSM_D_INNER, SSM_D_INNER + SSM_CONV_DIM], axis=-1)
    xbc = jax.nn.silu(causal_dwconv(xbc, conv_w, conv_b))
    xs, bm, cm = jnp.split(xbc, [SSM_D_INNER, SSM_D_INNER + SSM_GROUPS * SSM_STATE], axis=-1)
    xs = xs.reshape(bsz, length, SSM_HEADS, SSM_HEAD_DIM)
    bm = bm.reshape(bsz, length, SSM_GROUPS, SSM_STATE)
    cm = cm.reshape(bsz, length, SSM_GROUPS, SSM_STATE)
    dt = jax.nn.softplus((dt + dt_bias).astype(jnp.float32))
    dA = dt * (-jnp.exp(a_log.astype(jnp.float32)))

    def pad(t):
        return jnp.pad(t, [(0, 0), (SSM_PAD, 0)] + [(0, 0)] * (t.ndim - 2))

    y = ssd_scan(pad(xs * dt[..., None]), pad(dA), pad(bm), pad(cm))[:, SSM_PAD:]
    y = (y + xs * d_skip[:, None]).reshape(bsz, length, SSM_D_INNER)
    yg = (y * jax.nn.silu(z)).reshape(bsz, length, SSM_GROUPS, SSM_D_INNER // SSM_GROUPS)
    ygf = yg.astype(jnp.float32)
    ygf = ygf * lax.rsqrt(jnp.mean(jnp.square(ygf), axis=-1, keepdims=True) + RMS_EPS)
    yn = ygf.reshape(bsz, length, SSM_D_INNER).astype(u.dtype) * norm_g
    return yn @ w_out


def conv_ffn(u, w_in, conv_w, conv_b, w_out):
    hid = causal_dwconv(u @ w_in, conv_w, conv_b)
    gate, up = jnp.split(hid, [D_FF], axis=-1)
    return (jax.nn.silu(gate) * up) @ w_out


def mla_shared_kv(hs, w_down, norm_g, w_up, cos, sin):
    bsz, length, _ = hs.shape
    kv_a = hs @ w_down
    c_kv = rms_norm(kv_a[..., :KV_LORA], norm_g)
    k_pe = rope(kv_a[..., KV_LORA:], cos, sin)
    kv = (c_kv @ w_up).reshape(bsz, length, MLA_HEADS, QK_NOPE + V_HEAD)
    k_nope, v = kv[..., :QK_NOPE], kv[..., QK_NOPE:]
    padt = lambda t: jnp.pad(t, [(0, 0), (ATT_PAD, 0)] + [(0, 0)] * (t.ndim - 2))
    return padt(k_nope), padt(k_pe), padt(v)


def mla_attention(hs, w_dq, q_norm_g, w_uq, w_o, k_nope, k_pe, v, cos, sin):
    bsz, length, _ = hs.shape
    c_q = rms_norm(hs @ w_dq, q_norm_g)
    q = (c_q @ w_uq).reshape(bsz, length, MLA_HEADS, QK_NOPE + QK_ROPE)
    q_nope = q[..., :QK_NOPE]
    q_pe = rope(q[..., QK_NOPE:], cos[:, None], sin[:, None])
    lp = length + ATT_PAD
    nb = lp // Q_BLOCK

    def to_blocks(t):
        t = jnp.pad(t, ((0, 0), (ATT_PAD, 0), (0, 0), (0, 0)))
        return t.reshape(bsz, nb, Q_BLOCK, MLA_HEADS, t.shape[-1]).swapaxes(0, 1)

    k_pos = jnp.arange(lp)
    k_chunk = k_pos // CHUNK
    k_valid = k_pos >= ATT_PAD
    scale = (QK_NOPE + QK_ROPE) ** -0.5

    def attend_block(args):
        qn, qp, blk = args
        s = (jnp.einsum("bqhd,bkhd->bhqk", qn, k_nope)
             + jnp.einsum("bqhd,bkd->bhqk", qp, k_pe)).astype(jnp.float32) * scale
        q_chunk = (blk * Q_BLOCK + jnp.arange(Q_BLOCK)) // CHUNK
        mask = (k_chunk[None, :] <= q_chunk[:, None]) & k_valid[None, :]
        prob = jax.nn.softmax(jnp.where(mask, s, -1e30), axis=-1)
        return jnp.einsum("bhqk,bkhd->bqhd", prob.astype(v.dtype), v)

    o = lax.map(attend_block, (to_blocks(q_nope), to_blocks(q_pe), jnp.arange(nb)))
    o = o.swapaxes(0, 1).reshape(bsz, lp, MLA_HEADS * V_HEAD)[:, ATT_PAD:]
    return o @ w_o


def setup_inputs(seed: int = 0) -> dict:
    key = jax.random.key(seed)
    ks = iter(jax.random.split(key, 32))
    f32 = jnp.float32

    def dense(shape, fan_in, scale=1.0):
        return jax.random.normal(next(ks), shape, f32) * (fan_in ** -0.5) * scale

    def near_one(shape):
        return 1.0 + 0.02 * jax.random.normal(next(ks), shape, f32)

    def small(shape, s=0.01):
        return s * jax.random.normal(next(ks), shape, f32)

    na, nbl = N_A_LAYERS, N_B_LAYERS
    x = jax.random.normal(next(ks), (BATCH, SEQ, D_MODEL), f32)
    meta_tokens = jax.random.normal(next(ks), (N_META, D_MODEL), f32)
    m_w_in = dense((na, D_MODEL, SSM_IN_DIM), D_MODEL)
    m_conv_w = dense((na, SSM_CONV, SSM_CONV_DIM), SSM_CONV)
    m_conv_b = small((na, SSM_CONV_DIM))
    dt0 = jnp.exp(jax.random.uniform(next(ks), (na, SSM_HEADS), f32,
                                     math.log(1e-3), math.log(1e-1)))
    m_dt_bias = dt0 + jnp.log(-jnp.expm1(-dt0))
    m_a_log = jnp.log(jax.random.uniform(next(ks), (na, SSM_HEADS), f32, 1.0, 16.0))
    m_d = 1.0 + 0.1 * jax.random.normal(next(ks), (na, SSM_HEADS), f32)
    m_norm_g = near_one((na, SSM_D_INNER))
    m_w_out = dense((na, SSM_D_INNER, D_MODEL), SSM_D_INNER, DN_BETA)
    kv_w_down = dense((D_MODEL, KV_LORA + QK_ROPE), D_MODEL)
    kv_norm_g = near_one((KV_LORA,))
    kv_w_up = dense((KV_LORA, MLA_HEADS * (QK_NOPE + V_HEAD)), KV_LORA)
    q_w_down = dense((nbl, D_MODEL, Q_LORA), D_MODEL)
    q_norm_g = near_one((nbl, Q_LORA))
    q_w_up = dense((nbl, Q_LORA, MLA_HEADS * (QK_NOPE + QK_ROPE)), Q_LORA)
    o_w = dense((nbl, MLA_HEADS * V_HEAD, D_MODEL), MLA_HEADS * V_HEAD, DN_BETA)
    f_w_in = dense((DEPTH, D_MODEL, 2 * D_FF), D_MODEL)
    f_conv_w = dense((DEPTH, FFN_CONV, 2 * D_FF), FFN_CONV)
    f_conv_b = small((DEPTH, 2 * D_FF))
    f_w_out = dense((DEPTH, D_FF, D_MODEL), D_FF, DN_BETA)
    ln_g = near_one((DEPTH, 2, D_MODEL))
    ln_b = small((DEPTH, 2, D_MODEL), 0.02)
    return {"x": x, "meta_tokens": meta_tokens,
            "m_w_in": m_w_in, "m_conv_w": m_conv_w, "m_conv_b": m_conv_b,
            "m_dt_bias": m_dt_bias, "m_a_log": m_a_log, "m_d": m_d,
            "m_norm_g": m_norm_g, "m_w_out": m_w_out,
            "kv_w_down": kv_w_down, "kv_norm_g": kv_norm_g, "kv_w_up": kv_w_up,
            "q_w_down": q_w_down, "q_norm_g": q_norm_g, "q_w_up": q_w_up, "o_w": o_w,
            "f_w_in": f_w_in, "f_conv_w": f_conv_w, "f_conv_b": f_conv_b, "f_w_out": f_w_out,
            "ln_g": ln_g, "ln_b": ln_b}


def reference(x, meta_tokens, m_w_in, m_conv_w, m_conv_b, m_dt_bias, m_a_log, m_d,
              m_norm_g, m_w_out, kv_w_down, kv_norm_g, kv_w_up, q_w_down, q_norm_g,
              q_w_up, o_w, f_w_in, f_conv_w, f_conv_b, f_w_out, ln_g, ln_b):
    bsz = x.shape[0]
    meta = jnp.broadcast_to(meta_tokens[None].astype(x.dtype), (bsz, N_META, D_MODEL))
    h = jnp.concatenate([meta, x], axis=1)
    cos, sin = rope_tables(h.shape[1])
    k_nope = k_pe = v = None
    for layer in range(DEPTH):
        if layer < N_A_LAYERS:
            i = layer
            mix = mamba2_mixer(h, m_w_in[i], m_conv_w[i], m_conv_b[i], m_dt_bias[i],
                               m_a_log[i], m_d[i], m_norm_g[i], m_w_out[i])
        else:
            j = layer - N_A_LAYERS
            if j == 0:
                k_nope, k_pe, v = mla_shared_kv(h, kv_w_down, kv_norm_g, kv_w_up, cos, sin)
            mix = mla_attention(h, q_w_down[j], q_norm_g[j], q_w_up[j], o_w[j],
                                k_nope, k_pe, v, cos, sin)
        h = layer_norm(DN_ALPHA * h + mix, ln_g[layer, 0], ln_b[layer, 0])
        ffn = conv_ffn(h, f_w_in[layer], f_conv_w[layer], f_conv_b[layer], f_w_out[layer])
        h = layer_norm(DN_ALPHA * h + ffn, ln_g[layer, 1], ln_b[layer, 1])
    return h[:, N_META:]
```

```python
import functools
import math

import jax
import jax.numpy as jnp
from jax import lax
from jax.experimental import pallas as pl
from jax.experimental.pallas import tpu as pltpu

F32 = jnp.float32
BF16 = jnp.bfloat16

D_MODEL = 1024
N_META = 16
CHUNK = 64
DEPTH = 2
DN_ALPHA = (2.0 * DEPTH) ** 0.25
LN_EPS = 1e-5
RMS_EPS = 1e-6

SSM_D_INNER = 2 * D_MODEL
SSM_HEAD_DIM = 64
SSM_HEADS = SSM_D_INNER // SSM_HEAD_DIM
SSM_GROUPS = 4
SSM_STATE = 128
SSM_CONV = 4
SSM_BC = SSM_GROUPS * SSM_STATE
SSM_CONV_DIM = SSM_D_INNER + 2 * SSM_BC
GROUP_COLS = SSM_D_INNER // SSM_GROUPS
HEAD_LANES = 128

MLA_HEADS = 16
Q_LORA = 384
KV_LORA = 256
QK_NOPE = 64
QK_ROPE = 32
V_HEAD = 64
ROPE_THETA = 10000.0
HEAD_PAD = 128

D_FF = 2816
FFN_CONV = 3
FF_CHUNK = 256
N_FF_CHUNKS = D_FF // FF_CHUNK

PAD = 112
ROW_TILE = 384
HALO = 8
NEG_BIG = -1e30
VMEM_LIMIT = 56 * 1024 * 1024


def _const_spec(shape):
    nd = len(shape)
    return pl.BlockSpec(shape, lambda *_: (0,) * nd, pipeline_mode=pl.Buffered(1))


def _layer_norm(v, g, b):
    mu = jnp.mean(v, axis=-1, keepdims=True)
    d = v - mu
    var = jnp.mean(d * d, axis=-1, keepdims=True)
    return d * lax.rsqrt(var + LN_EPS) * g + b


def _rms(v):
    return v * lax.rsqrt(jnp.mean(v * v, axis=-1, keepdims=True) + RMS_EPS)


def _silu(v):
    return v / (1.0 + jnp.exp(-v))


def _split_bf16(v, parts):
    out = []
    rem = v
    for _ in range(parts):
        p = rem.astype(BF16)
        out.append(p)
        rem = rem - p.astype(F32)
    return out


def _dot(a, b):
    return jnp.dot(a, b, preferred_element_type=F32)


def _dot_split(lhs_parts, rhs):
    acc = _dot(lhs_parts[0], rhs)
    for p in lhs_parts[1:]:
        acc = acc + _dot(p, rhs)
    return acc


def _mamba_front_kernel(h_ref, wz_ref, wxbc_ref, wdt_ref, cw_ref, cb_ref, dtb_ref, alog_ref,
                        z_ref, xbc_ref, dt_ref, da_ref, buf_ref):
    t = pl.program_id(1)
    rows = ROW_TILE
    u = h_ref[0].astype(BF16)
    z_ref[0] = _dot(u, wz_ref[...]).astype(BF16)

    @pl.when(t == 0)
    def _():
        buf_ref[0:HALO, :] = jnp.zeros((HALO, SSM_CONV_DIM), F32)

    buf_ref[HALO:HALO + rows, :] = _dot(u, wxbc_ref[...])

    @pl.when(t == 0)
    def _():
        buf_ref[HALO:HALO + PAD, :] = jnp.zeros((PAD, SSM_CONV_DIM), F32)

    acc = cb_ref[...]
    for k in range(SSM_CONV):
        off = HALO - (SSM_CONV - 1) + k
        acc = acc + cw_ref[k:k + 1, :] * buf_ref[off:off + rows, :]
    xbc_ref[0] = _silu(acc).astype(BF16)
    buf_ref[0:HALO, :] = buf_ref[rows:rows + HALO, :]

    x = _dot(u, wdt_ref[...]) + dtb_ref[...]
    dt = jnp.maximum(x, 0.0) + jnp.log1p(jnp.exp(-jnp.abs(x)))
    dt_ref[0] = dt
    da_ref[0] = dt * (-jnp.exp(alog_ref[...]))

    @pl.when(t == 0)
    def _():
        xbc_ref[0, 0:PAD, :] = jnp.zeros((PAD, SSM_CONV_DIM), BF16)
        dt_ref[0, 0:PAD, :] = jnp.zeros((PAD, HEAD_LANES), F32)
        da_ref[0, 0:PAD, :] = jnp.zeros((PAD, HEAD_LANES), F32)


def _mamba_front(h, wz, wxbc, wdt, cw, cb, dtb, alog):
    bsz, lp, _ = h.shape
    nt = lp // ROW_TILE
    row = lambda w: pl.BlockSpec((1, ROW_TILE, w), lambda b, t: (b, t, 0))
    return pl.pallas_call(
        _mamba_front_kernel,
        grid=(bsz, nt),
        in_specs=[row(D_MODEL), _const_spec(wz.shape), _const_spec(wxbc.shape),
                  _const_spec(wdt.shape), _const_spec(cw.shape), _const_spec(cb.shape),
                  _const_spec(dtb.shape), _const_spec(alog.shape)],
        out_specs=[row(SSM_D_INNER), row(SSM_CONV_DIM), row(HEAD_LANES), row(HEAD_LANES)],
        out_shape=[jax.ShapeDtypeStruct((bsz, lp, SSM_D_INNER), BF16),
                   jax.ShapeDtypeStruct((bsz, lp, SSM_CONV_DIM), BF16),
                   jax.ShapeDtypeStruct((bsz, lp, HEAD_LANES), F32),
                   jax.ShapeDtypeStruct((bsz, lp, HEAD_LANES), F32)],
        scratch_shapes=[pltpu.VMEM((ROW_TILE + HALO, SSM_CONV_DIM), F32)],
        compiler_params=pltpu.CompilerParams(
            dimension_semantics=("arbitrary", "arbitrary"), vmem_limit_bytes=VMEM_LIMIT),
        name="mamba_front",
    )(h, wz, wxbc, wdt, cw, cb, dtb, alog)


def _mamba_scan_kernel(h_ref, z_ref, xs_ref, b_ref, c_ref, dt_ref, da_ref, e_ref, dskip_ref,
                       ng_ref, wout_ref, lng_ref, lnb_ref, out_ref,
                       acs_ref, dte_ref, y_ref, state_ref):
    t = pl.program_id(1)
    rows = ROW_TILE
    n_chunks = rows // CHUNK

    @pl.when(t == 0)
    def _():
        state_ref[...] = jnp.zeros(state_ref.shape, F32)

    r_i = lax.broadcasted_iota(jnp.int32, (rows, rows), 0)
    c_i = lax.broadcasted_iota(jnp.int32, (rows, rows), 1)
    tri = jnp.where(((r_i >> 6) == (c_i >> 6)) & (c_i <= r_i), 1.0, 0.0).astype(BF16)
    acs = sum(_dot(tri, p) for p in _split_bf16(da_ref[0], 3))
    acs_ref[...] = _dot_split(_split_bf16(acs, 3), e_ref[...])
    dte_ref[...] = _dot_split(_split_bf16(dt_ref[0], 2), e_ref[...])

    s_i = lax.broadcasted_iota(jnp.int32, (CHUNK, 2 * CHUNK), 0)
    l_i = lax.broadcasted_iota(jnp.int32, (CHUNK, 2 * CHUNK), 1)
    diag = (l_i & (CHUNK - 1)) == s_i
    causal = (l_i & (CHUNK - 1)) <= s_i
    br_i = lax.broadcasted_iota(jnp.int32, (2 * CHUNK, 2 * CHUNK), 0)
    bc_i = lax.broadcasted_iota(jnp.int32, (2 * CHUNK, 2 * CHUNK), 1)
    blockdiag = (br_i >> 6) == (bc_i >> 6)

    def chunk_body(c, carry):
        r0 = pl.multiple_of(c * CHUNK, CHUNK)
        acs_c = acs_ref[pl.ds(r0, CHUNK), :]
        dte_c = dte_ref[pl.ds(r0, CHUNK), :]
        xs_c = xs_ref[0, pl.ds(r0, CHUNK), :]
        b_c = b_ref[0, pl.ds(r0, CHUNK), :]
        c_c = c_ref[0, pl.ds(r0, CHUNK), :]
        last = acs_c[CHUNK - 1:CHUNK, :]
        xw = (xs_c.astype(F32) * dte_c * jnp.exp(last - acs_c)).astype(BF16)
        grow = jnp.exp(acs_c)
        chunk_decay = jnp.exp(last)
        for g in range(SSM_GROUPS):
            gl = slice(g * SSM_STATE, (g + 1) * SSM_STATE)
            gc = slice(g * GROUP_COLS, (g + 1) * GROUP_COLS)
            b_g = b_c[:, gl]
            c_g = c_c[:, gl]
            cb2 = lax.dot_general(c_g, jnp.concatenate([b_g, b_g], axis=0),
                                  (((1,), (1,)), ((), ())), preferred_element_type=F32)
            st = state_ref[g]
            y_off = _dot(c_g, st.astype(BF16)) * grow[:, gc]
            for jj in range(GROUP_COLS // (2 * SSM_HEAD_DIM)):
                lo = g * GROUP_COLS + jj * 2 * SSM_HEAD_DIM
                pl_ = slice(lo, lo + 2 * SSM_HEAD_DIM)
                a_p = acs_c[:, pl_]
                a_row = jnp.sum(jnp.where(diag, a_p, 0.0), axis=0, keepdims=True)
                dt_row = jnp.sum(jnp.where(diag, dte_c[:, pl_], 0.0), axis=0, keepdims=True)
                decay = jnp.exp(jnp.where(causal, a_p - a_row, NEG_BIG))
                m2 = (cb2 * decay * dt_row).astype(BF16)
                x_p = xs_c[:, pl_]
                x_bd = jnp.where(blockdiag, jnp.concatenate([x_p, x_p], axis=0),
                                 jnp.zeros((), BF16))
                y_ref[pl.ds(r0, CHUNK), pl_] = (
                    _dot(m2, x_bd) + y_off[:, jj * 2 * SSM_HEAD_DIM:(jj + 1) * 2 * SSM_HEAD_DIM])
            s_new = lax.dot_general(b_g, xw[:, gc], (((0,), (0,)), ((), ())),
                                    preferred_element_type=F32)
            state_ref[g] = st * chunk_decay[:, gc] + s_new
        return carry

    lax.fori_loop(0, n_chunks, chunk_body, 0)

    y = y_ref[...] + xs_ref[0].astype(F32) * dskip_ref[...]
    yg = y * _silu(z_ref[0].astype(F32))
    parts = [_rms(yg[:, g * GROUP_COLS:(g + 1) * GROUP_COLS]) for g in range(SSM_GROUPS)]
    yn = jnp.concatenate(parts, axis=1) * ng_ref[...]
    mix = _dot(yn.astype(BF16), wout_ref[...])
    out_ref[0] = _layer_norm(DN_ALPHA * h_ref[0] + mix, lng_ref[...], lnb_ref[...])


def _mamba_scan(h, z, xbc, dt, da, e, dskip, ng, wout, lng, lnb):
    bsz, lp, _ = h.shape
    nt = lp // ROW_TILE
    row = lambda w, cb=0: pl.BlockSpec((1, ROW_TILE, w), lambda b, t: (b, t, cb))
    return pl.pallas_call(
        _mamba_scan_kernel,
        grid=(bsz, nt),
        in_specs=[row(D_MODEL), row(SSM_D_INNER), row(SSM_D_INNER),
                  row(SSM_BC, SSM_D_INNER // SSM_BC), row(SSM_BC, SSM_D_INNER // SSM_BC + 1),
                  row(HEAD_LANES), row(HEAD_LANES),
                  _const_spec(e.shape), _const_spec(dskip.shape), _const_spec(ng.shape),
                  _const_spec(wout.shape), _const_spec(lng.shape), _const_spec(lnb.shape)],
        out_specs=row(D_MODEL),
        out_shape=jax.ShapeDtypeStruct((bsz, lp, D_MODEL), F32),
        scratch_shapes=[pltpu.VMEM((ROW_TILE, SSM_D_INNER), F32),
                        pltpu.VMEM((ROW_TILE, SSM_D_INNER), F32),
                        pltpu.VMEM((ROW_TILE, SSM_D_INNER), F32),
                        pltpu.VMEM((SSM_GROUPS, SSM_STATE, GROUP_COLS), F32)],
        compiler_params=pltpu.CompilerParams(
            dimension_semantics=("arbitrary", "arbitrary"), vmem_limit_bytes=VMEM_LIMIT),
        name="mamba_scan",
    )(h, z, xbc, xbc, xbc, dt, da, e, dskip, ng, wout, lng, lnb)


def _ffn_body(h, t, win_ref, cw_ref, cb_ref, wout_ref, lng_ref, lnb_ref, out_ref,
              halo_ref, buf_ref):
    rows = ROW_TILE
    width = 2 * FF_CHUNK
    u = h.astype(BF16)

    @pl.when(t == 0)
    def _():
        halo_ref[...] = jnp.zeros(halo_ref.shape, F32)

    acc = jnp.zeros((rows, D_MODEL), F32)
    for cb in range(N_FF_CHUNKS):
        cs = slice(cb * width, (cb + 1) * width)
        buf_ref[0:HALO, :] = halo_ref[:, cs]
        buf_ref[HALO:HALO + rows, :] = _dot(u, win_ref[:, cs])

        @pl.when(t == 0)
        def _():
            buf_ref[HALO:HALO + PAD, :] = jnp.zeros((PAD, width), F32)

        hid = cb_ref[:, cs]
        for k in range(FFN_CONV):
            off = HALO - (FFN_CONV - 1) + k
            hid = hid + cw_ref[k:k + 1, cs] * buf_ref[off:off + rows, :]
        halo_ref[:, cs] = buf_ref[rows:rows + HALO, :]
        act = (_silu(hid[:, :FF_CHUNK]) * hid[:, FF_CHUNK:]).astype(BF16)
        acc = acc + _dot(act, wout_ref[cb * FF_CHUNK:(cb + 1) * FF_CHUNK, :])
    out_ref[0] = _layer_norm(DN_ALPHA * h + acc, lng_ref[...], lnb_ref[...])


def _ffn_kernel(h_ref, win_ref, cw_ref, cb_ref, wout_ref, lng_ref, lnb_ref, out_ref,
                halo_ref, buf_ref):
    _ffn_body(h_ref[0], pl.program_id(1), win_ref, cw_ref, cb_ref, wout_ref, lng_ref, lnb_ref,
              out_ref, halo_ref, buf_ref)


def _attn_out_ffn_kernel(h_ref, o_ref, wo_ref, lng0_ref, lnb0_ref, win_ref, cw_ref, cb_ref,
                         wout_ref, lng_ref, lnb_ref, out_ref, halo_ref, buf_ref):
    mix = _dot(o_ref[0], wo_ref[...])
    h = _layer_norm(DN_ALPHA * h_ref[0] + mix, lng0_ref[...], lnb0_ref[...])
    _ffn_body(h, pl.program_id(1), win_ref, cw_ref, cb_ref, wout_ref, lng_ref, lnb_ref,
              out_ref, halo_ref, buf_ref)


def _ffn(h, win, cw, cb, wout, lng, lnb, attn=None):
    bsz, lp, _ = h.shape
    nt = lp // ROW_TILE
    row = pl.BlockSpec((1, ROW_TILE, D_MODEL), lambda b, t: (b, t, 0))
    consts = (win, cw, cb, wout, lng, lnb)
    if attn is None:
        kern, args, specs = _ffn_kernel, (h,) + consts, [row]
    else:
        o, wo, lng0, lnb0 = attn
        kern = _attn_out_ffn_kernel
        args = (h, o, wo, lng0, lnb0) + consts
        specs = [row, row, _const_spec(wo.shape), _const_spec(lng0.shape),
                 _const_spec(lnb0.shape)]
    return pl.pallas_call(
        kern,
        grid=(bsz, nt),
        in_specs=specs + [_const_spec(a.shape) for a in consts],
        out_specs=row,
        out_shape=jax.ShapeDtypeStruct((bsz, lp, D_MODEL), F32),
        scratch_shapes=[pltpu.VMEM((HALO, 2 * D_FF), F32),
                        pltpu.VMEM((ROW_TILE + HALO, 2 * FF_CHUNK), F32)],
        compiler_params=pltpu.CompilerParams(
            dimension_semantics=("arbitrary", "arbitrary"), vmem_limit_bytes=VMEM_LIMIT),
        name="conv_ffn" if attn is None else "attn_out_conv_ffn",
    )(*args)


def _mla_proj_kernel(h_ref, wkvd_ref, kvg_ref, wk_ref, wv_ref, wpe_ref, wper_ref, cosk_ref,
                     sink_ref, wqd_ref, qg_ref, wq_ref, wqr_ref, cosq_ref, sinq_ref,
                     q_ref, k_ref, v_ref):
    u = h_ref[0].astype(BF16)
    ckv = (_rms(_dot(u, wkvd_ref[...])) * kvg_ref[...]).astype(BF16)
    k_all = _dot(ckv, wk_ref[...])
    v_all = _dot(ckv, wv_ref[...])
    k_pe = _dot(u, wpe_ref[...]) * cosk_ref[...] + _dot(u, wper_ref[...]) * sink_ref[...]
    lane = lax.broadcasted_iota(jnp.int32, (1, HEAD_PAD), 1)
    ones_col = jnp.where(lane == V_HEAD, 1.0, 0.0)
    cq = (_rms(_dot(u, wqd_ref[...])) * qg_ref[...]).astype(BF16)
    q_all = _dot(cq, wq_ref[...])
    q_rot = _dot(cq, wqr_ref[...])
    cosq = cosq_ref[...]
    sinq = sinq_ref[...]
    for hd in range(MLA_HEADS):
        hl = slice(hd * HEAD_PAD, (hd + 1) * HEAD_PAD)
        k_ref[0, hd] = (k_all[:, hl] + k_pe).astype(BF16)
        v_ref[0, hd] = (v_all[:, hl] + ones_col).astype(BF16)
        q_ref[0, hd] = (q_all[:, hl] * cosq + q_rot[:, hl] * sinq).astype(BF16)


def _mla_proj(h, wkvd, kvg, wk, wv, wpe, wper, cosk, sink, wqd, qg, wq, wqr, cosq, sinq):
    bsz, lp, _ = h.shape
    nt = lp // ROW_TILE
    row = pl.BlockSpec((1, ROW_TILE, D_MODEL), lambda b, t: (b, t, 0))
    tab = pl.BlockSpec((ROW_TILE, HEAD_PAD), lambda b, t: (t, 0))
    head = pl.BlockSpec((1, MLA_HEADS, ROW_TILE, HEAD_PAD), lambda b, t: (b, 0, t, 0))
    shape = jax.ShapeDtypeStruct((bsz, MLA_HEADS, lp, HEAD_PAD), BF16)
    c = _const_spec
    return pl.pallas_call(
        _mla_proj_kernel,
        grid=(bsz, nt),
        in_specs=[row, c(wkvd.shape), c(kvg.shape), c(wk.shape), c(wv.shape), c(wpe.shape),
                  c(wper.shape), tab, tab, c(wqd.shape), c(qg.shape), c(wq.shape),
                  c(wqr.shape), tab, tab],
        out_specs=[head, head, head],
        out_shape=[shape, shape, shape],
        compiler_params=pltpu.CompilerParams(
            dimension_semantics=("arbitrary", "arbitrary"), vmem_limit_bytes=VMEM_LIMIT),
        name="mla_proj",
    )(h, wkvd, kvg, wk, wv, wpe, wper, cosk, sink, wqd, qg, wq, wqr, cosq, sinq)


def _attention_kernel(q_ref, k_ref, v_ref, o_ref):
    i = pl.program_id(2)
    rows = ROW_TILE
    r_i = lax.broadcasted_iota(jnp.int32, (rows, rows), 0)
    c_i = lax.broadcasted_iota(jnp.int32, (rows, rows), 1)
    outs = []
    for hh in range(2):
        q = q_ref[0, hh]

        def step(j, carry, masked, hh=hh, q=q):
            m, acc = carry
            k0 = pl.multiple_of(j * rows, rows)
            s = lax.dot_general(q, k_ref[0, hh, pl.ds(k0, rows), :],
                                (((1,), (1,)), ((), ())), preferred_element_type=F32)
            if masked:
                kpos = c_i + j * rows
                q_chunk = (r_i + i * rows) >> 6
                if masked == "diag":
                    q_chunk = q_chunk - jnp.where(i > 0, 0, 1 << 20)
                ok = ((kpos >> 6) <= q_chunk) & (kpos >= PAD)
                s = jnp.where(ok, s, NEG_BIG)
            m_new = jnp.maximum(m, jnp.max(s, axis=-1, keepdims=True))
            p = jnp.exp2(s - m_new).astype(BF16)
            acc = jnp.exp2(m - m_new) * acc + _dot(p, v_ref[0, hh, pl.ds(k0, rows), :])
            return m_new, acc

        carry = (jnp.full((rows, 1), NEG_BIG, F32), jnp.zeros((rows, HEAD_PAD), F32))
        carry = step(0, carry, "first")
        carry = lax.fori_loop(1, i, lambda j, c: step(j, c, None), carry)
        _, acc = step(i, carry, "diag")
        outs.append(acc[:, :V_HEAD] / acc[:, V_HEAD:V_HEAD + 1])
    o_ref[0] = jnp.concatenate(outs, axis=1).astype(BF16)


def _attention(q, k, v):
    bsz, nh, lp, _ = q.shape
    nt = lp // ROW_TILE
    return pl.pallas_call(
        _attention_kernel,
        grid=(bsz, nh // 2, nt),
        in_specs=[pl.BlockSpec((1, 2, ROW_TILE, HEAD_PAD), lambda b, hp, i: (b, hp, i, 0)),
                  pl.BlockSpec((1, 2, lp, HEAD_PAD), lambda b, hp, i: (b, hp, 0, 0)),
                  pl.BlockSpec((1, 2, lp, HEAD_PAD), lambda b, hp, i: (b, hp, 0, 0))],
        out_specs=pl.BlockSpec((1, ROW_TILE, 2 * V_HEAD), lambda b, hp, i: (b, i, hp)),
        out_shape=jax.ShapeDtypeStruct((bsz, lp, MLA_HEADS * V_HEAD), BF16),
        compiler_params=pltpu.CompilerParams(
            dimension_semantics=("arbitrary", "arbitrary", "arbitrary"),
            vmem_limit_bytes=VMEM_LIMIT),
        name="mla_attention",
    )(q, k, v)


def _row(v, width=None):
    v = v.astype(F32).reshape(1, -1)
    if width is not None and v.shape[1] < width:
        v = jnp.pad(v, ((0, 0), (0, width - v.shape[1])))
    return v


def _ffn_params(w_in, conv_w, conv_b, w_out):
    def interleave(a):
        lead = a.shape[:-1]
        a = a.reshape(lead + (2, N_FF_CHUNKS, FF_CHUNK))
        return jnp.swapaxes(a, -3, -2).reshape(lead + (2 * D_FF,))
    return (interleave(w_in).astype(BF16), interleave(conv_w).astype(F32),
            interleave(conv_b.reshape(1, -1)).astype(F32), w_out.astype(BF16))


def _rope_tables(lp):
    half = QK_ROPE // 2
    pos = jnp.maximum(jnp.arange(lp, dtype=F32) - PAD, 0.0)
    inv_freq = ROPE_THETA ** (-jnp.arange(half, dtype=F32) * 2.0 / QK_ROPE)
    ang = pos[:, None] * inv_freq[None, :]
    cos, sin = jnp.cos(ang), jnp.sin(ang)
    zeros = lambda n: jnp.zeros((lp, n), F32)
    tail = HEAD_PAD - QK_NOPE - QK_ROPE
    cosk = jnp.concatenate([zeros(QK_NOPE), cos, cos, zeros(tail)], axis=1)
    sink = jnp.concatenate([zeros(QK_NOPE), sin, sin, zeros(tail)], axis=1)
    c = (QK_NOPE + QK_ROPE) ** -0.5 * math.log2(math.e)
    cosq = c * jnp.concatenate([jnp.ones((lp, QK_NOPE), F32), cos, cos, zeros(tail)], axis=1)
    return cosk, sink, cosq, c * sink


def _rot_half_cols(w):
    half = QK_ROPE // 2
    return jnp.concatenate([-w[..., half:], w[..., :half]], axis=-1)


def _pad_last(a, before, total):
    pads = [(0, 0)] * (a.ndim - 1) + [(before, total - before - a.shape[-1])]
    return jnp.pad(a, pads)


def kernel(x, meta_tokens, m_w_in, m_conv_w, m_conv_b, m_dt_bias, m_a_log, m_d, m_norm_g, m_w_out, kv_w_down, kv_norm_g, kv_w_up, q_w_down, q_norm_g, q_w_up, o_w, f_w_in, f_conv_w, f_conv_b, f_w_out, ln_g, ln_b):
    bsz, seq, _ = x.shape
    lp = PAD + N_META + seq
    assert lp % ROW_TILE == 0 and (PAD + N_META) % CHUNK == 0
    meta = jnp.broadcast_to(meta_tokens[None].astype(x.dtype), (bsz, N_META, D_MODEL))
    h = jnp.concatenate([jnp.zeros((bsz, PAD, D_MODEL), x.dtype), meta, x], axis=1)

    w_in = m_w_in[0]
    wz = w_in[:, :SSM_D_INNER].astype(BF16)
    wxbc = w_in[:, SSM_D_INNER:SSM_D_INNER + SSM_CONV_DIM].astype(BF16)
    wdt = _pad_last(w_in[:, SSM_D_INNER + SSM_CONV_DIM:], 0, HEAD_LANES).astype(BF16)
    z, xbc, dt, da = _mamba_front(h, wz, wxbc, wdt, m_conv_w[0].astype(F32), _row(m_conv_b[0]),
                                  _row(m_dt_bias[0], HEAD_LANES), _row(m_a_log[0], HEAD_LANES))
    head_of_col = jnp.arange(SSM_D_INNER) // SSM_HEAD_DIM
    expand = (jnp.arange(HEAD_LANES)[:, None] == head_of_col[None, :]).astype(BF16)
    h = _mamba_scan(h, z, xbc, dt, da, expand, _row(jnp.repeat(m_d[0], SSM_HEAD_DIM)),
                    _row(m_norm_g[0]), m_w_out[0].astype(BF16), _row(ln_g[0, 0]), _row(ln_b[0, 0]))
    h = _ffn(h, *_ffn_params(f_w_in[0], f_conv_w[0], f_conv_b[0], f_w_out[0]),
             _row(ln_g[0, 1]), _row(ln_b[0, 1]))

    cosk, sink, cosq, sinq = _rope_tables(lp)
    w_rope = kv_w_down[:, KV_LORA:]
    wpe = _pad_last(w_rope, QK_NOPE, HEAD_PAD).astype(BF16)
    wper = _pad_last(_rot_half_cols(w_rope), QK_NOPE, HEAD_PAD).astype(BF16)
    kv_up = kv_w_up.reshape(KV_LORA, MLA_HEADS, QK_NOPE + V_HEAD)
    wk = _pad_last(kv_up[..., :QK_NOPE], 0, HEAD_PAD).reshape(KV_LORA, -1).astype(BF16)
    wv = _pad_last(kv_up[..., QK_NOPE:], 0, HEAD_PAD).reshape(KV_LORA, -1).astype(BF16)
    q_up = q_w_up[0].reshape(Q_LORA, MLA_HEADS, QK_NOPE + QK_ROPE)
    wq = _pad_last(q_up, 0, HEAD_PAD).reshape(Q_LORA, -1).astype(BF16)
    wqr = _pad_last(_rot_half_cols(q_up[..., QK_NOPE:]), QK_NOPE, HEAD_PAD)
    wqr = wqr.reshape(Q_LORA, -1).astype(BF16)
    q, k, v = _mla_proj(h, kv_w_down[:, :KV_LORA].astype(BF16), _row(kv_norm_g), wk, wv, wpe,
                        wper, cosk, sink, q_w_down[0].astype(BF16), _row(q_norm_g[0]), wq, wqr,
                        cosq, sinq)
    o = _attention(q, k, v)
    h = _ffn(h, *_ffn_params(f_w_in[1], f_conv_w[1], f_conv_b[1], f_w_out[1]),
             _row(ln_g[1, 1]), _row(ln_b[1, 1]),
             attn=(o, o_w[0].astype(BF16), _row(ln_g[1, 0]), _row(ln_b[1, 0])))
    return h[:, PAD + N_META:]
```

```python
import functools
import math

import jax
import jax.numpy as jnp
from jax import lax
from jax.experimental import pallas as pl
from jax.experimental.pallas import tpu as pltpu

F32 = jnp.float32
BF16 = jnp.bfloat16

D_MODEL = 1024
N_META = 16
CHUNK = 64
DEPTH = 2
DN_ALPHA = (2.0 * DEPTH) ** 0.25
LN_EPS = 1e-5
RMS_EPS = 1e-6

SSM_D_INNER = 2 * D_MODEL
SSM_HEAD_DIM = 64
SSM_HEADS = SSM_D_INNER // SSM_HEAD_DIM
SSM_GROUPS = 4
SSM_STATE = 128
SSM_CONV = 4
SSM_BC = SSM_GROUPS * SSM_STATE
SSM_CONV_DIM = SSM_D_INNER + 2 * SSM_BC
GROUP_COLS = SSM_D_INNER // SSM_GROUPS
HEAD_LANES = 128

MLA_HEADS = 16
Q_LORA = 384
KV_LORA = 256
QK_NOPE = 64
QK_ROPE = 32
V_HEAD = 64
ROPE_THETA = 10000.0
HEAD_PAD = 128
ATT_HEADS = 8
Q_ROWS = 384

D_FF = 2816
FFN_CONV = 3
FF_CHUNK = 256
N_FF_CHUNKS = D_FF // FF_CHUNK

PAD = 112
ROW_TILE = 384
HALO = 8
NEG_BIG = -1e30
VMEM_LIMIT = 56 * 1024 * 1024


def _const_spec(shape):
    nd = len(shape)
    return pl.BlockSpec(shape, lambda *_: (0,) * nd, pipeline_mode=pl.Buffered(1))


def _layer_norm(v, g, b):
    mu = jnp.mean(v, axis=-1, keepdims=True)
    d = v - mu
    var = jnp.mean(d * d, axis=-1, keepdims=True)
    return d * lax.rsqrt(var + LN_EPS) * g + b


def _rms(v):
    return v * lax.rsqrt(jnp.mean(v * v, axis=-1, keepdims=True) + RMS_EPS)


def _silu(v):
    return v / (1.0 + jnp.exp(-v))


def _split_bf16(v, parts):
    out = []
    rem = v
    for _ in range(parts):
        p = rem.astype(BF16)
        out.append(p)
        rem = rem - p.astype(F32)
    return out


def _dot(a, b):
    return jnp.dot(a, b, preferred_element_type=F32)


def _dot_split(lhs_parts, rhs):
    acc = _dot(lhs_parts[0], rhs)
    for p in lhs_parts[1:]:
        acc = acc + _dot(p, rhs)
    return acc


def _mamba_front_kernel(h_ref, wz_ref, wxbc_ref, wdt_ref, cw_ref, cb_ref, dtb_ref, alog_ref,
                        z_ref, xbc_ref, dt_ref, da_ref, buf_ref):
    t = pl.program_id(1)
    rows = ROW_TILE
    u = h_ref[0].astype(BF16)
    z_ref[0] = _dot(u, wz_ref[...]).astype(BF16)

    @pl.when(t == 0)
    def _():
        buf_ref[0:HALO, :] = jnp.zeros((HALO, SSM_CONV_DIM), F32)

    buf_ref[HALO:HALO + rows, :] = _dot(u, wxbc_ref[...])

    @pl.when(t == 0)
    def _():
        buf_ref[HALO:HALO + PAD, :] = jnp.zeros((PAD, SSM_CONV_DIM), F32)

    acc = cb_ref[...]
    for k in range(SSM_CONV):
        off = HALO - (SSM_CONV - 1) + k
        acc = acc + cw_ref[k:k + 1, :] * buf_ref[off:off + rows, :]
    xbc_ref[0] = _silu(acc).astype(BF16)
    buf_ref[0:HALO, :] = buf_ref[rows:rows + HALO, :]

    x = _dot(u, wdt_ref[...]) + dtb_ref[...]
    dt = jnp.maximum(x, 0.0) + jnp.log1p(jnp.exp(-jnp.abs(x)))
    dt_ref[0] = dt
    da_ref[0] = dt * (-jnp.exp(alog_ref[...]))

    @pl.when(t == 0)
    def _():
        xbc_ref[0, 0:PAD, :] = jnp.zeros((PAD, SSM_CONV_DIM), BF16)
        dt_ref[0, 0:PAD, :] = jnp.zeros((PAD, HEAD_LANES), F32)
        da_ref[0, 0:PAD, :] = jnp.zeros((PAD, HEAD_LANES), F32)


def _mamba_front(h, wz, wxbc, wdt, cw, cb, dtb, alog):
    bsz, lp, _ = h.shape
    nt = lp // ROW_TILE
    row = lambda w: pl.BlockSpec((1, ROW_TILE, w), lambda b, t: (b, t, 0))
    return pl.pallas_call(
        _mamba_front_kernel,
        grid=(bsz, nt),
        in_specs=[row(D_MODEL), _const_spec(wz.shape), _const_spec(wxbc.shape),
                  _const_spec(wdt.shape), _const_spec(cw.shape), _const_spec(cb.shape),
                  _const_spec(dtb.shape), _const_spec(alog.shape)],
        out_specs=[row(SSM_D_INNER), row(SSM_CONV_DIM), row(HEAD_LANES), row(HEAD_LANES)],
        out_shape=[jax.ShapeDtypeStruct((bsz, lp, SSM_D_INNER), BF16),
                   jax.ShapeDtypeStruct((bsz, lp, SSM_CONV_DIM), BF16),
                   jax.ShapeDtypeStruct((bsz, lp, HEAD_LANES), F32),
                   jax.ShapeDtypeStruct((bsz, lp, HEAD_LANES), F32)],
        scratch_shapes=[pltpu.VMEM((ROW_TILE + HALO, SSM_CONV_DIM), F32)],
        compiler_params=pltpu.CompilerParams(
            dimension_semantics=("arbitrary", "arbitrary"), vmem_limit_bytes=VMEM_LIMIT),
        name="mamba_front",
    )(h, wz, wxbc, wdt, cw, cb, dtb, alog)


def _mamba_scan_kernel(h_ref, z_ref, xs_ref, b_ref, c_ref, dt_ref, da_ref, e_ref, dskip_ref,
                       ng_ref, wout_ref, lng_ref, lnb_ref, out_ref,
                       acs_ref, dte_ref, y_ref, state_ref):
    t = pl.program_id(1)
    rows = ROW_TILE
    n_chunks = rows // CHUNK

    @pl.when(t == 0)
    def _():
        state_ref[...] = jnp.zeros(state_ref.shape, F32)

    r_i = lax.broadcasted_iota(jnp.int32, (rows, rows), 0)
    c_i = lax.broadcasted_iota(jnp.int32, (rows, rows), 1)
    tri = jnp.where(((r_i >> 6) == (c_i >> 6)) & (c_i <= r_i), 1.0, 0.0).astype(BF16)
    acs = sum(_dot(tri, p) for p in _split_bf16(da_ref[0], 3))
    acs_ref[...] = _dot_split(_split_bf16(acs, 3), e_ref[...])
    dte_ref[...] = _dot_split(_split_bf16(dt_ref[0], 2), e_ref[...])

    s_i = lax.broadcasted_iota(jnp.int32, (CHUNK, 2 * CHUNK), 0)
    l_i = lax.broadcasted_iota(jnp.int32, (CHUNK, 2 * CHUNK), 1)
    diag = (l_i & (CHUNK - 1)) == s_i
    causal = (l_i & (CHUNK - 1)) <= s_i
    br_i = lax.broadcasted_iota(jnp.int32, (2 * CHUNK, 2 * CHUNK), 0)
    bc_i = lax.broadcasted_iota(jnp.int32, (2 * CHUNK, 2 * CHUNK), 1)
    blockdiag = (br_i >> 6) == (bc_i >> 6)

    def chunk_body(c, carry):
        r0 = pl.multiple_of(c * CHUNK, CHUNK)
        acs_c = acs_ref[pl.ds(r0, CHUNK), :]
        dte_c = dte_ref[pl.ds(r0, CHUNK), :]
        xs_c = xs_ref[0, pl.ds(r0, CHUNK), :]
        b_c = b_ref[0, pl.ds(r0, CHUNK), :]
        c_c = c_ref[0, pl.ds(r0, CHUNK), :]
        last = acs_c[CHUNK - 1:CHUNK, :]
        xw = (xs_c.astype(F32) * dte_c * jnp.exp(last - acs_c)).astype(BF16)
        grow = jnp.exp(acs_c)
        chunk_decay = jnp.exp(last)
        for g in range(SSM_GROUPS):
            gl = slice(g * SSM_STATE, (g + 1) * SSM_STATE)
            gc = slice(g * GROUP_COLS, (g + 1) * GROUP_COLS)
            b_g = b_c[:, gl]
            c_g = c_c[:, gl]
            cb2 = lax.dot_general(c_g, jnp.concatenate([b_g, b_g], axis=0),
                                  (((1,), (1,)), ((), ())), preferred_element_type=F32)
            st = state_ref[g]
            y_off = _dot(c_g, st.astype(BF16)) * grow[:, gc]
            for jj in range(GROUP_COLS // (2 * SSM_HEAD_DIM)):
                lo = g * GROUP_COLS + jj * 2 * SSM_HEAD_DIM
                pl_ = slice(lo, lo + 2 * SSM_HEAD_DIM)
                a_p = acs_c[:, pl_]
                a_row = jnp.sum(jnp.where(diag, a_p, 0.0), axis=0, keepdims=True)
                dt_row = jnp.sum(jnp.where(diag, dte_c[:, pl_], 0.0), axis=0, keepdims=True)
                decay = jnp.exp(jnp.where(causal, a_p - a_row, NEG_BIG))
                m2 = (cb2 * decay * dt_row).astype(BF16)
                x_p = xs_c[:, pl_]
                x_bd = jnp.where(blockdiag, jnp.concatenate([x_p, x_p], axis=0),
                                 jnp.zeros((), BF16))
                y_ref[pl.ds(r0, CHUNK), pl_] = (
                    _dot(m2, x_bd) + y_off[:, jj * 2 * SSM_HEAD_DIM:(jj + 1) * 2 * SSM_HEAD_DIM])
            s_new = lax.dot_general(b_g, xw[:, gc], (((0,), (0,)), ((), ())),
                                    preferred_element_type=F32)
            state_ref[g] = st * chunk_decay[:, gc] + s_new
        return carry

    lax.fori_loop(0, n_chunks, chunk_body, 0)

    y = y_ref[...] + xs_ref[0].astype(F32) * dskip_ref[...]
    yg = y * _silu(z_ref[0].astype(F32))
    parts = [_rms(yg[:, g * GROUP_COLS:(g + 1) * GROUP_COLS]) for g in range(SSM_GROUPS)]
    yn = jnp.concatenate(parts, axis=1) * ng_ref[...]
    mix = _dot(yn.astype(BF16), wout_ref[...])
    out_ref[0] = _layer_norm(DN_ALPHA * h_ref[0] + mix, lng_ref[...], lnb_ref[...])


def _mamba_scan(h, z, xbc, dt, da, e, dskip, ng, wout, lng, lnb):
    bsz, lp, _ = h.shape
    nt = lp // ROW_TILE
    row = lambda w, cb=0: pl.BlockSpec((1, ROW_TILE, w), lambda b, t: (b, t, cb))
    return pl.pallas_call(
        _mamba_scan_kernel,
        grid=(bsz, nt),
        in_specs=[row(D_MODEL), row(SSM_D_INNER), row(SSM_D_INNER),
                  row(SSM_BC, SSM_D_INNER // SSM_BC), row(SSM_BC, SSM_D_INNER // SSM_BC + 1),
                  row(HEAD_LANES), row(HEAD_LANES),
                  _const_spec(e.shape), _const_spec(dskip.shape), _const_spec(ng.shape),
                  _const_spec(wout.shape), _const_spec(lng.shape), _const_spec(lnb.shape)],
        out_specs=row(D_MODEL),
        out_shape=jax.ShapeDtypeStruct((bsz, lp, D_MODEL), F32),
        scratch_shapes=[pltpu.VMEM((ROW_TILE, SSM_D_INNER), F32),
                        pltpu.VMEM((ROW_TILE, SSM_D_INNER), F32),
                        pltpu.VMEM((ROW_TILE, SSM_D_INNER), F32),
                        pltpu.VMEM((SSM_GROUPS, SSM_STATE, GROUP_COLS), F32)],
        compiler_params=pltpu.CompilerParams(
            dimension_semantics=("arbitrary", "arbitrary"), vmem_limit_bytes=VMEM_LIMIT),
        name="mamba_scan",
    )(h, z, xbc, xbc, xbc, dt, da, e, dskip, ng, wout, lng, lnb)


def _ffn_body(h, t, win_ref, cw_ref, cb_ref, wout_ref, lng_ref, lnb_ref, out_ref,
              halo_ref, buf_ref):
    rows = ROW_TILE
    width = 2 * FF_CHUNK
    u = h.astype(BF16)

    @pl.when(t == 0)
    def _():
        halo_ref[...] = jnp.zeros(halo_ref.shape, F32)

    acc = jnp.zeros((rows, D_MODEL), F32)
    for cb in range(N_FF_CHUNKS):
        cs = slice(cb * width, (cb + 1) * width)
        buf_ref[0:HALO, :] = halo_ref[:, cs]
        buf_ref[HALO:HALO + rows, :] = _dot(u, win_ref[:, cs])

        @pl.when(t == 0)
        def _():
            buf_ref[HALO:HALO + PAD, :] = jnp.zeros((PAD, width), F32)

        hid = cb_ref[:, cs]
        for k in range(FFN_CONV):
            off = HALO - (FFN_CONV - 1) + k
            hid = hid + cw_ref[k:k + 1, cs] * buf_ref[off:off + rows, :]
        halo_ref[:, cs] = buf_ref[rows:rows + HALO, :]
        act = (_silu(hid[:, :FF_CHUNK]) * hid[:, FF_CHUNK:]).astype(BF16)
        acc = acc + _dot(act, wout_ref[cb * FF_CHUNK:(cb + 1) * FF_CHUNK, :])
    out_ref[0] = _layer_norm(DN_ALPHA * h + acc, lng_ref[...], lnb_ref[...])


def _ffn_kernel(h_ref, win_ref, cw_ref, cb_ref, wout_ref, lng_ref, lnb_ref, out_ref,
                halo_ref, buf_ref):
    _ffn_body(h_ref[0], pl.program_id(1), win_ref, cw_ref, cb_ref, wout_ref, lng_ref, lnb_ref,
              out_ref, halo_ref, buf_ref)


def _attn_out_ffn_kernel(h_ref, o_ref, wo_ref, lng0_ref, lnb0_ref, win_ref, cw_ref, cb_ref,
                         wout_ref, lng_ref, lnb_ref, out_ref, halo_ref, buf_ref):
    mix = _dot(o_ref[0], wo_ref[...])
    h = _layer_norm(DN_ALPHA * h_ref[0] + mix, lng0_ref[...], lnb0_ref[...])
    _ffn_body(h, pl.program_id(1), win_ref, cw_ref, cb_ref, wout_ref, lng_ref, lnb_ref,
              out_ref, halo_ref, buf_ref)


def _ffn(h, win, cw, cb, wout, lng, lnb, attn=None):
    bsz, lp, _ = h.shape
    nt = lp // ROW_TILE
    row = pl.BlockSpec((1, ROW_TILE, D_MODEL), lambda b, t: (b, t, 0))
    consts = (win, cw, cb, wout, lng, lnb)
    if attn is None:
        kern, args, specs = _ffn_kernel, (h,) + consts, [row]
    else:
        o, wo, lng0, lnb0 = attn
        kern = _attn_out_ffn_kernel
        args = (h, o, wo, lng0, lnb0) + consts
        specs = [row, row, _const_spec(wo.shape), _const_spec(lng0.shape),
                 _const_spec(lnb0.shape)]
    return pl.pallas_call(
        kern,
        grid=(bsz, nt),
        in_specs=specs + [_const_spec(a.shape) for a in consts],
        out_specs=row,
        out_shape=jax.ShapeDtypeStruct((bsz, lp, D_MODEL), F32),
        scratch_shapes=[pltpu.VMEM((HALO, 2 * D_FF), F32),
                        pltpu.VMEM((ROW_TILE + HALO, 2 * FF_CHUNK), F32)],
        compiler_params=pltpu.CompilerParams(
            dimension_semantics=("arbitrary", "arbitrary"), vmem_limit_bytes=VMEM_LIMIT),
        name="conv_ffn" if attn is None else "attn_out_conv_ffn",
    )(*args)


def _mla_proj_kernel(h_ref, wkvd_ref, kvg_ref, wk_ref, wv_ref, wpe_ref, wper_ref, cosk_ref,
                     sink_ref, wqd_ref, qg_ref, wq_ref, wqr_ref, cosq_ref, sinq_ref,
                     q_ref, k_ref, v_ref):
    u = h_ref[0].astype(BF16)
    ckv = (_rms(_dot(u, wkvd_ref[...])) * kvg_ref[...]).astype(BF16)
    k_all = _dot(ckv, wk_ref[...])
    v_all = _dot(ckv, wv_ref[...])
    k_pe = _dot(u, wpe_ref[...]) * cosk_ref[...] + _dot(u, wper_ref[...]) * sink_ref[...]
    lane = lax.broadcasted_iota(jnp.int32, (1, HEAD_PAD), 1)
    ones_col = jnp.where(lane == V_HEAD, 1.0, 0.0)
    cq = (_rms(_dot(u, wqd_ref[...])) * qg_ref[...]).astype(BF16)
    q_all = _dot(cq, wq_ref[...])
    q_rot = _dot(cq, wqr_ref[...])
    cosq = cosq_ref[...]
    sinq = sinq_ref[...]
    for hd in range(MLA_HEADS):
        hl = slice(hd * HEAD_PAD, (hd + 1) * HEAD_PAD)
        k_ref[0, hd] = (k_all[:, hl] + k_pe).astype(BF16)
        v_ref[0, hd] = (v_all[:, hl] + ones_col).astype(BF16)
        q_ref[0, hd] = (q_all[:, hl] * cosq + q_rot[:, hl] * sinq).astype(BF16)


def _mla_proj(h, wkvd, kvg, wk, wv, wpe, wper, cosk, sink, wqd, qg, wq, wqr, cosq, sinq):
    bsz, lp, _ = h.shape
    nt = lp // ROW_TILE
    row = pl.BlockSpec((1, ROW_TILE, D_MODEL), lambda b, t: (b, t, 0))
    tab = pl.BlockSpec((ROW_TILE, HEAD_PAD), lambda b, t: (t, 0))
    head = pl.BlockSpec((1, MLA_HEADS, ROW_TILE, HEAD_PAD), lambda b, t: (b, 0, t, 0))
    shape = jax.ShapeDtypeStruct((bsz, MLA_HEADS, lp, HEAD_PAD), BF16)
    c = _const_spec
    return pl.pallas_call(
        _mla_proj_kernel,
        grid=(bsz, nt),
        in_specs=[row, c(wkvd.shape), c(kvg.shape), c(wk.shape), c(wv.shape), c(wpe.shape),
                  c(wper.shape), tab, tab, c(wqd.shape), c(qg.shape), c(wq.shape),
                  c(wqr.shape), tab, tab],
        out_specs=[head, head, head],
        out_shape=[shape, shape, shape],
        compiler_params=pltpu.CompilerParams(
            dimension_semantics=("arbitrary", "arbitrary"), vmem_limit_bytes=VMEM_LIMIT),
        name="mla_proj",
    )(h, wkvd, kvg, wk, wv, wpe, wper, cosk, sink, wqd, qg, wq, wqr, cosq, sinq)


def _attention_kernel(q_ref, k_ref, v_ref, o_ref):
    i = pl.program_id(2)
    rows = ROW_TILE
    r_i = lax.broadcasted_iota(jnp.int32, (rows, rows), 0)
    c_i = lax.broadcasted_iota(jnp.int32, (rows, rows), 1)
    n_rb = rows // Q_ROWS
    qs = [[q_ref[0, hh, rb * Q_ROWS:(rb + 1) * Q_ROWS, :] for rb in range(n_rb)]
          for hh in range(ATT_HEADS)]

    def step(j, carry, masked):
        k0 = pl.multiple_of(j * rows, rows)
        if masked:
            kpos = c_i + j * rows
            q_chunk = (r_i + i * rows) >> 6
            if masked == "diag":
                q_chunk = q_chunk - jnp.where(i > 0, 0, 1 << 20)
            ok = ((kpos >> 6) <= q_chunk) & (kpos >= PAD)
        scores = []
        for hh in range(ATT_HEADS):
            k_t = k_ref[0, hh, pl.ds(k0, rows), :]
            for rb in range(n_rb):
                s = lax.dot_general(qs[hh][rb], k_t, (((1,), (1,)), ((), ())),
                                    preferred_element_type=F32)
                if masked:
                    s = jnp.where(ok[rb * Q_ROWS:(rb + 1) * Q_ROWS], s, NEG_BIG)
                scores.append(s)
        out = []
        for hh in range(ATT_HEADS):
            v_t = v_ref[0, hh, pl.ds(k0, rows), :]
            for rb in range(n_rb):
                m, acc = carry[hh * n_rb + rb]
                s = scores[hh * n_rb + rb]
                m_new = jnp.maximum(m, jnp.max(s, axis=-1, keepdims=True))
                p = jnp.exp2(s - m_new).astype(BF16)
                out.append((m_new, jnp.exp2(m - m_new) * acc + _dot(p, v_t)))
        return tuple(out)

    carry = tuple((jnp.full((Q_ROWS, 1), NEG_BIG, F32), jnp.zeros((Q_ROWS, HEAD_PAD), F32))
                  for _ in range(ATT_HEADS * n_rb))
    carry = step(0, carry, "first")
    carry = lax.fori_loop(1, i, lambda j, c: step(j, c, None), carry)
    carry = step(i, carry, "diag")
    outs = []
    for hh in range(ATT_HEADS):
        accs = [carry[hh * n_rb + rb][1] for rb in range(n_rb)]
        outs.append(jnp.concatenate(
            [a[:, :V_HEAD] / a[:, V_HEAD:V_HEAD + 1] for a in accs], axis=0))
    o_ref[0] = jnp.concatenate(outs, axis=1).astype(BF16)


def _attention(q, k, v):
    bsz, nh, lp, _ = q.shape
    nt = lp // ROW_TILE
    hs = ATT_HEADS
    return pl.pallas_call(
        _attention_kernel,
        grid=(bsz, nh // hs, nt),
        in_specs=[pl.BlockSpec((1, hs, ROW_TILE, HEAD_PAD), lambda b, hp, i: (b, hp, i, 0)),
                  pl.BlockSpec((1, hs, lp, HEAD_PAD), lambda b, hp, i: (b, hp, 0, 0)),
                  pl.BlockSpec((1, hs, lp, HEAD_PAD), lambda b, hp, i: (b, hp, 0, 0))],
        out_specs=pl.BlockSpec((1, ROW_TILE, hs * V_HEAD), lambda b, hp, i: (b, i, hp)),
        out_shape=jax.ShapeDtypeStruct((bsz, lp, MLA_HEADS * V_HEAD), BF16),
        compiler_params=pltpu.CompilerParams(
            dimension_semantics=("arbitrary", "arbitrary", "arbitrary"),
            vmem_limit_bytes=VMEM_LIMIT),
        name="mla_attention",
    )(q, k, v)


def _row(v, width=None):
    v = v.astype(F32).reshape(1, -1)
    if width is not None and v.shape[1] < width:
        v = jnp.pad(v, ((0, 0), (0, width - v.shape[1])))
    return v


def _ffn_params(w_in, conv_w, conv_b, w_out):
    def interleave(a):
        lead = a.shape[:-1]
        a = a.reshape(lead + (2, N_FF_CHUNKS, FF_CHUNK))
        return jnp.swapaxes(a, -3, -2).reshape(lead + (2 * D_FF,))
    return (interleave(w_in).astype(BF16), interleave(conv_w).astype(F32),
            interleave(conv_b.reshape(1, -1)).astype(F32), w_out.astype(BF16))


def _rope_tables(lp):
    half = QK_ROPE // 2
    pos = jnp.maximum(jnp.arange(lp, dtype=F32) - PAD, 0.0)
    inv_freq = ROPE_THETA ** (-jnp.arange(half, dtype=F32) * 2.0 / QK_ROPE)
    ang = pos[:, None] * inv_freq[None, :]
    cos, sin = jnp.cos(ang), jnp.sin(ang)
    zeros = lambda n: jnp.zeros((lp, n), F32)
    tail = HEAD_PAD - QK_NOPE - QK_ROPE
    cosk = jnp.concatenate([zeros(QK_NOPE), cos, cos, zeros(tail)], axis=1)
    sink = jnp.concatenate([zeros(QK_NOPE), sin, sin, zeros(tail)], axis=1)
    c = (QK_NOPE + QK_ROPE) ** -0.5 * math.log2(math.e)
    cosq = c * jnp.concatenate([jnp.ones((lp, QK_NOPE), F32), cos, cos, zeros(tail)], axis=1)
    return cosk, sink, cosq, c * sink


def _rot_half_cols(w):
    half = QK_ROPE // 2
    return jnp.concatenate([-w[..., half:], w[..., :half]], axis=-1)


def _pad_last(a, before, total):
    pads = [(0, 0)] * (a.ndim - 1) + [(before, total - before - a.shape[-1])]
    return jnp.pad(a, pads)


def kernel(x, meta_tokens, m_w_in, m_conv_w, m_conv_b, m_dt_bias, m_a_log, m_d, m_norm_g, m_w_out, kv_w_down, kv_norm_g, kv_w_up, q_w_down, q_norm_g, q_w_up, o_w, f_w_in, f_conv_w, f_conv_b, f_w_out, ln_g, ln_b):
    bsz, seq, _ = x.shape
    lp = PAD + N_META + seq
    assert lp % ROW_TILE == 0 and (PAD + N_META) % CHUNK == 0
    meta = jnp.broadcast_to(meta_tokens[None].astype(x.dtype), (bsz, N_META, D_MODEL))
    h = jnp.concatenate([jnp.zeros((bsz, PAD, D_MODEL), x.dtype), meta, x], axis=1)

    w_in = m_w_in[0]
    wz = w_in[:, :SSM_D_INNER].astype(BF16)
    wxbc = w_in[:, SSM_D_INNER:SSM_D_INNER + SSM_CONV_DIM].astype(BF16)
    wdt = _pad_last(w_in[:, SSM_D_INNER + SSM_CONV_DIM:], 0, HEAD_LANES).astype(BF16)
    z, xbc, dt, da = _mamba_front(h, wz, wxbc, wdt, m_conv_w[0].astype(F32), _row(m_conv_b[0]),
                                  _row(m_dt_bias[0], HEAD_LANES), _row(m_a_log[0], HEAD_LANES))
    head_of_col = jnp.arange(SSM_D_INNER) // SSM_HEAD_DIM
    expand = (jnp.arange(HEAD_LANES)[:, None] == head_of_col[None, :]).astype(BF16)
    h = _mamba_scan(h, z, xbc, dt, da, expand, _row(jnp.repeat(m_d[0], SSM_HEAD_DIM)),
                    _row(m_norm_g[0]), m_w_out[0].astype(BF16), _row(ln_g[0, 0]), _row(ln_b[0, 0]))
    h = _ffn(h, *_ffn_params(f_w_in[0], f_conv_w[0], f_conv_b[0], f_w_out[0]),
             _row(ln_g[0, 1]), _row(ln_b[0, 1]))

    cosk, sink, cosq, sinq = _rope_tables(lp)
    w_rope = kv_w_down[:, KV_LORA:]
    wpe = _pad_last(w_rope, QK_NOPE, HEAD_PAD).astype(BF16)
    wper = _pad_last(_rot_half_cols(w_rope), QK_NOPE, HEAD_PAD).astype(BF16)
    kv_up = kv_w_up.reshape(KV_LORA, MLA_HEADS, QK_NOPE + V_HEAD)
    wk = _pad_last(kv_up[..., :QK_NOPE], 0, HEAD_PAD).reshape(KV_LORA, -1).astype(BF16)
    wv = _pad_last(kv_up[..., QK_NOPE:], 0, HEAD_PAD).reshape(KV_LORA, -1).astype(BF16)
    q_up = q_w_up[0].reshape(Q_LORA, MLA_HEADS, QK_NOPE + QK_ROPE)
    wq = _pad_last(q_up, 0, HEAD_PAD).reshape(Q_LORA, -1).astype(BF16)
    wqr = _pad_last(_rot_half_cols(q_up[..., QK_NOPE:]), QK_NOPE, HEAD_PAD)
    wqr = wqr.reshape(Q_LORA, -1).astype(BF16)
    q, k, v = _mla_proj(h, kv_w_down[:, :KV_LORA].astype(BF16), _row(kv_norm_g), wk, wv, wpe,
                        wper, cosk, sink, q_w_down[0].astype(BF16), _row(q_norm_g[0]), wq, wqr,
                        cosq, sinq)
    o = _attention(q, k, v)
    h = _ffn(h, *_ffn_params(f_w_in[1], f_conv_w[1], f_conv_b[1], f_w_out[1]),
             _row(ln_g[1, 1]), _row(ln_b[1, 1]),
             attn=(o, o_w[0].astype(BF16), _row(ln_g[1, 0]), _row(ln_b[1, 0])))
    return h[:, PAD + N_META:]
```

```python
import functools
import math

import jax
import jax.numpy as jnp
from jax import lax
from jax.experimental import pallas as pl
from jax.experimental.pallas import tpu as pltpu

F32 = jnp.float32
BF16 = jnp.bfloat16

D_MODEL = 1024
N_META = 16
CHUNK = 64
DEPTH = 2
DN_ALPHA = (2.0 * DEPTH) ** 0.25
LN_EPS = 1e-5
RMS_EPS = 1e-6

SSM_D_INNER = 2 * D_MODEL
SSM_HEAD_DIM = 64
SSM_HEADS = SSM_D_INNER // SSM_HEAD_DIM
SSM_GROUPS = 4
SSM_STATE = 128
SSM_CONV = 4
SSM_BC = SSM_GROUPS * SSM_STATE
SSM_CONV_DIM = SSM_D_INNER + 2 * SSM_BC
GROUP_COLS = SSM_D_INNER // SSM_GROUPS
HEAD_LANES = 128
FRONT_CHUNK = 512
FRONT_BUFFERS = 3

MLA_HEADS = 16
Q_LORA = 384
KV_LORA = 256
QK_NOPE = 64
QK_ROPE = 32
V_HEAD = 64
ROPE_THETA = 10000.0
HEAD_PAD = 128
ATT_HEADS = 8
Q_ROWS = 384

D_FF = 2816
FFN_CONV = 3
FF_CHUNK = 256
N_FF_CHUNKS = D_FF // FF_CHUNK
FF_BUFFERS = 3

PAD = 112
ROW_TILE = 384
HALO = 8
NEG_BIG = -1e30
VMEM_LIMIT = 56 * 1024 * 1024


def _const_spec(shape):
    nd = len(shape)
    return pl.BlockSpec(shape, lambda *_: (0,) * nd, pipeline_mode=pl.Buffered(1))


def _layer_norm(v, g, b):
    mu = jnp.mean(v, axis=-1, keepdims=True)
    d = v - mu
    var = jnp.mean(d * d, axis=-1, keepdims=True)
    return d * lax.rsqrt(var + LN_EPS) * g + b


def _rms(v):
    return v * lax.rsqrt(jnp.mean(v * v, axis=-1, keepdims=True) + RMS_EPS)


def _silu(v):
    return v / (1.0 + jnp.exp(-v))


def _split_bf16(v, parts):
    out = []
    rem = v
    for _ in range(parts):
        p = rem.astype(BF16)
        out.append(p)
        rem = rem - p.astype(F32)
    return out


def _dot(a, b):
    return jnp.dot(a, b, preferred_element_type=F32)


def _dot_split(lhs_parts, rhs):
    acc = _dot(lhs_parts[0], rhs)
    for p in lhs_parts[1:]:
        acc = acc + _dot(p, rhs)
    return acc


def _mamba_front_kernel(h_ref, wz_ref, wxbc_ref, wdt_ref, cw_ref, cb_ref, dtb_ref, alog_ref,
                        z_ref, xbc_ref, dt_ref, da_ref, halo_ref, buf_ref):
    t = pl.program_id(1)
    rows = ROW_TILE
    width = FRONT_CHUNK
    n_x = SSM_CONV_DIM // width
    n_z = SSM_D_INNER // width
    u = h_ref[0].astype(BF16)

    @pl.when(t == 0)
    def _():
        halo_ref[...] = jnp.zeros(halo_ref.shape, F32)

    def project(c):
        cs = slice(c * width, (c + 1) * width)
        slot = c % FRONT_BUFFERS
        buf_ref[slot, 0:HALO, :] = halo_ref[:, cs]
        pre = _dot(u, wxbc_ref[:, cs])
        buf_ref[slot, HALO:HALO + rows, :] = pre
        buf_ref[slot, HALO:HALO + PAD, :] = jnp.where(t > 0, pre[:PAD], 0.0)

    def gate(c):
        cs = slice(c * width, (c + 1) * width)
        z_ref[0, :, cs] = _dot(u, wz_ref[:, cs]).astype(BF16)

    for c in range(FRONT_BUFFERS - 1):
        project(c)
    for c in range(n_x):
        if c + FRONT_BUFFERS - 1 < n_x:
            project(c + FRONT_BUFFERS - 1)
        if c < n_z:
            gate(c)
        cs = slice(c * width, (c + 1) * width)
        slot = c % FRONT_BUFFERS
        acc = cb_ref[:, cs]
        for k in range(SSM_CONV):
            off = HALO - (SSM_CONV - 1) + k
            acc = acc + cw_ref[k:k + 1, cs] * buf_ref[slot, off:off + rows, :]
        halo_ref[:, cs] = buf_ref[slot, rows:rows + HALO, :]
        act = _silu(acc).astype(BF16)
        xbc_ref[0, :, cs] = act
        xbc_ref[0, 0:PAD, cs] = jnp.where(t > 0, act[:PAD], jnp.zeros((), BF16))

    x = _dot(u, wdt_ref[...]) + dtb_ref[...]
    dt = jnp.maximum(x, 0.0) + jnp.log1p(jnp.exp(-jnp.abs(x)))
    da = dt * (-jnp.exp(alog_ref[...]))
    dt_ref[0] = dt
    da_ref[0] = da
    dt_ref[0, 0:PAD, :] = jnp.where(t > 0, dt[:PAD], 0.0)
    da_ref[0, 0:PAD, :] = jnp.where(t > 0, da[:PAD], 0.0)


def _mamba_front(h, wz, wxbc, wdt, cw, cb, dtb, alog):
    bsz, lp, _ = h.shape
    nt = lp // ROW_TILE
    row = lambda w: pl.BlockSpec((1, ROW_TILE, w), lambda b, t: (b, t, 0))
    return pl.pallas_call(
        _mamba_front_kernel,
        grid=(bsz, nt),
        in_specs=[row(D_MODEL), _const_spec(wz.shape), _const_spec(wxbc.shape),
                  _const_spec(wdt.shape), _const_spec(cw.shape), _const_spec(cb.shape),
                  _const_spec(dtb.shape), _const_spec(alog.shape)],
        out_specs=[row(SSM_D_INNER), row(SSM_CONV_DIM), row(HEAD_LANES), row(HEAD_LANES)],
        out_shape=[jax.ShapeDtypeStruct((bsz, lp, SSM_D_INNER), BF16),
                   jax.ShapeDtypeStruct((bsz, lp, SSM_CONV_DIM), BF16),
                   jax.ShapeDtypeStruct((bsz, lp, HEAD_LANES), F32),
                   jax.ShapeDtypeStruct((bsz, lp, HEAD_LANES), F32)],
        scratch_shapes=[pltpu.VMEM((HALO, SSM_CONV_DIM), F32),
                        pltpu.VMEM((FRONT_BUFFERS, ROW_TILE + HALO, FRONT_CHUNK), F32)],
        compiler_params=pltpu.CompilerParams(
            dimension_semantics=("arbitrary", "arbitrary"), vmem_limit_bytes=VMEM_LIMIT),
        name="mamba_front",
    )(h, wz, wxbc, wdt, cw, cb, dtb, alog)


def _mamba_scan_kernel(h_ref, z_ref, xs_ref, b_ref, c_ref, dt_ref, da_ref, e_ref, dskip_ref,
                       ng_ref, wout_ref, lng_ref, lnb_ref, out_ref,
                       acs_ref, dte_ref, y_ref, state_ref):
    t = pl.program_id(1)
    rows = ROW_TILE
    n_chunks = rows // CHUNK

    @pl.when(t == 0)
    def _():
        state_ref[...] = jnp.zeros(state_ref.shape, F32)

    r_i = lax.broadcasted_iota(jnp.int32, (rows, rows), 0)
    c_i = lax.broadcasted_iota(jnp.int32, (rows, rows), 1)
    tri = jnp.where(((r_i >> 6) == (c_i >> 6)) & (c_i <= r_i), 1.0, 0.0).astype(BF16)
    acs = sum(_dot(tri, p) for p in _split_bf16(da_ref[0], 3))
    lane_copy = lax.broadcasted_iota(jnp.int32, (rows, HEAD_LANES), 1) // SSM_HEADS

    def packed_terms(v, parts):
        terms = _split_bf16(v, parts)
        out = jnp.zeros((rows, HEAD_LANES), BF16)
        for k, term in enumerate(terms):
            out = jnp.where(lane_copy == k, term, out)
        return out

    acs_ref[...] = _dot(packed_terms(acs, 3), e_ref[...])
    dte_ref[...] = _dot(packed_terms(dt_ref[0], 2), e_ref[...])

    s_i = lax.broadcasted_iota(jnp.int32, (CHUNK, 2 * CHUNK), 0)
    l_i = lax.broadcasted_iota(jnp.int32, (CHUNK, 2 * CHUNK), 1)
    diag = (l_i & (CHUNK - 1)) == s_i
    causal = (l_i & (CHUNK - 1)) <= s_i
    br_i = lax.broadcasted_iota(jnp.int32, (2 * CHUNK, 2 * CHUNK), 0)
    bc_i = lax.broadcasted_iota(jnp.int32, (2 * CHUNK, 2 * CHUNK), 1)
    blockdiag = (br_i >> 6) == (bc_i >> 6)

    def chunk_body(c, carry):
        r0 = pl.multiple_of(c * CHUNK, CHUNK)
        acs_c = acs_ref[pl.ds(r0, CHUNK), :]
        dte_c = dte_ref[pl.ds(r0, CHUNK), :]
        xs_c = xs_ref[0, pl.ds(r0, CHUNK), :]
        b_c = b_ref[0, pl.ds(r0, CHUNK), :]
        c_c = c_ref[0, pl.ds(r0, CHUNK), :]
        last = acs_c[CHUNK - 1:CHUNK, :]
        groups = [slice(g * SSM_STATE, (g + 1) * SSM_STATE) for g in range(SSM_GROUPS)]
        gcols = [slice(g * GROUP_COLS, (g + 1) * GROUP_COLS) for g in range(SSM_GROUPS)]
        pair = 2 * SSM_HEAD_DIM
        states = [state_ref[g] for g in range(SSM_GROUPS)]
        cb2 = [lax.dot_general(c_c[:, gl], jnp.concatenate([b_c[:, gl], b_c[:, gl]], axis=0),
                               (((1,), (1,)), ((), ())), preferred_element_type=F32)
               for gl in groups]
        y_off = [_dot(c_c[:, gl], st.astype(BF16)) for gl, st in zip(groups, states)]
        x_dt = xs_c.astype(F32) * dte_c
        xw = (x_dt * jnp.exp(last - acs_c)).astype(BF16)
        x_dt = x_dt.astype(BF16)
        s_new = [lax.dot_general(b_c[:, gl], xw[:, gc], (((0,), (0,)), ((), ())),
                                 preferred_element_type=F32) for gl, gc in zip(groups, gcols)]
        grow = jnp.exp(acs_c)
        chunk_decay = jnp.exp(last)
        for g in range(SSM_GROUPS):
            for jj in range(GROUP_COLS // pair):
                pl_ = slice(g * GROUP_COLS + jj * pair, g * GROUP_COLS + (jj + 1) * pair)
                a_p = acs_c[:, pl_]
                a_row = jnp.sum(jnp.where(diag, a_p, 0.0), axis=0, keepdims=True)
                decay = jnp.exp(jnp.where(causal, a_p - a_row, NEG_BIG))
                m2 = (cb2[g] * decay).astype(BF16)
                x_p = x_dt[:, pl_]
                x_bd = jnp.where(blockdiag, jnp.concatenate([x_p, x_p], axis=0),
                                 jnp.zeros((), BF16))
                y_ref[pl.ds(r0, CHUNK), pl_] = (
                    _dot(m2, x_bd) + y_off[g][:, jj * pair:(jj + 1) * pair] * grow[:, pl_])
            state_ref[g] = states[g] * chunk_decay[:, gcols[g]] + s_new[g]
        return carry

    lax.fori_loop(0, n_chunks, chunk_body, 0)

    y = y_ref[...] + xs_ref[0].astype(F32) * dskip_ref[...]
    yg = y * _silu(z_ref[0].astype(F32))
    parts = [_rms(yg[:, g * GROUP_COLS:(g + 1) * GROUP_COLS]) for g in range(SSM_GROUPS)]
    yn = jnp.concatenate(parts, axis=1) * ng_ref[...]
    mix = _dot(yn.astype(BF16), wout_ref[...])
    out_ref[0] = _layer_norm(DN_ALPHA * h_ref[0] + mix, lng_ref[...], lnb_ref[...])


def _mamba_scan(h, z, xbc, dt, da, e, dskip, ng, wout, lng, lnb):
    bsz, lp, _ = h.shape
    nt = lp // ROW_TILE
    row = lambda w, cb=0: pl.BlockSpec((1, ROW_TILE, w), lambda b, t: (b, t, cb))
    return pl.pallas_call(
        _mamba_scan_kernel,
        grid=(bsz, nt),
        in_specs=[row(D_MODEL), row(SSM_D_INNER), row(SSM_D_INNER),
                  row(SSM_BC, SSM_D_INNER // SSM_BC), row(SSM_BC, SSM_D_INNER // SSM_BC + 1),
                  row(HEAD_LANES), row(HEAD_LANES),
                  _const_spec(e.shape), _const_spec(dskip.shape), _const_spec(ng.shape),
                  _const_spec(wout.shape), _const_spec(lng.shape), _const_spec(lnb.shape)],
        out_specs=row(D_MODEL),
        out_shape=jax.ShapeDtypeStruct((bsz, lp, D_MODEL), F32),
        scratch_shapes=[pltpu.VMEM((ROW_TILE, SSM_D_INNER), F32),
                        pltpu.VMEM((ROW_TILE, SSM_D_INNER), F32),
                        pltpu.VMEM((ROW_TILE, SSM_D_INNER), F32),
                        pltpu.VMEM((SSM_GROUPS, SSM_STATE, GROUP_COLS), F32)],
        compiler_params=pltpu.CompilerParams(
            dimension_semantics=("arbitrary", "arbitrary"), vmem_limit_bytes=VMEM_LIMIT),
        name="mamba_scan",
    )(h, z, xbc, xbc, xbc, dt, da, e, dskip, ng, wout, lng, lnb)


def _ffn_body(h, t, win_ref, cw_ref, cb_ref, wout_ref, lng_ref, lnb_ref, out_ref,
              halo_ref, buf_ref):
    rows = ROW_TILE
    u = h.astype(BF16)

    @pl.when(t == 0)
    def _():
        halo_ref[...] = jnp.zeros(halo_ref.shape, F32)

    def cols(cb):
        return (slice(cb * FF_CHUNK, (cb + 1) * FF_CHUNK),
                slice(D_FF + cb * FF_CHUNK, D_FF + (cb + 1) * FF_CHUNK))

    def project(cb):
        slot = cb % FF_BUFFERS
        for part, cs in enumerate(cols(cb)):
            buf_ref[slot, part, 0:HALO, :] = halo_ref[:, cs]
            pre = _dot(u, win_ref[:, cs])
            buf_ref[slot, part, HALO:HALO + rows, :] = pre
            buf_ref[slot, part, HALO:HALO + PAD, :] = jnp.where(t > 0, pre[:PAD], 0.0)

    def conv(cb, part, cs):
        slot = cb % FF_BUFFERS
        hid = cb_ref[:, cs]
        for k in range(FFN_CONV):
            off = HALO - (FFN_CONV - 1) + k
            hid = hid + cw_ref[k:k + 1, cs] * buf_ref[slot, part, off:off + rows, :]
        halo_ref[:, cs] = buf_ref[slot, part, rows:rows + HALO, :]
        return hid

    for cb in range(FF_BUFFERS - 1):
        project(cb)
    acc = jnp.zeros((rows, D_MODEL), F32)
    for cb in range(N_FF_CHUNKS):
        if cb + FF_BUFFERS - 1 < N_FF_CHUNKS:
            project(cb + FF_BUFFERS - 1)
        gate_cols, up_cols = cols(cb)
        act = (_silu(conv(cb, 0, gate_cols)) * conv(cb, 1, up_cols)).astype(BF16)
        acc = acc + _dot(act, wout_ref[cb * FF_CHUNK:(cb + 1) * FF_CHUNK, :])
    out_ref[0] = _layer_norm(DN_ALPHA * h + acc, lng_ref[...], lnb_ref[...])


def _ffn_kernel(h_ref, win_ref, cw_ref, cb_ref, wout_ref, lng_ref, lnb_ref, out_ref,
                halo_ref, buf_ref):
    _ffn_body(h_ref[0], pl.program_id(1), win_ref, cw_ref, cb_ref, wout_ref, lng_ref, lnb_ref,
              out_ref, halo_ref, buf_ref)


def _attn_out_ffn_kernel(h_ref, o_ref, wo_ref, lng0_ref, lnb0_ref, win_ref, cw_ref, cb_ref,
                         wout_ref, lng_ref, lnb_ref, out_ref, halo_ref, buf_ref):
    mix = _dot(o_ref[0], wo_ref[...])
    h = _layer_norm(DN_ALPHA * h_ref[0] + mix, lng0_ref[...], lnb0_ref[...])
    _ffn_body(h, pl.program_id(1), win_ref, cw_ref, cb_ref, wout_ref, lng_ref, lnb_ref,
              out_ref, halo_ref, buf_ref)


def _ffn(h, win, cw, cb, wout, lng, lnb, attn=None):
    bsz, lp, _ = h.shape
    nt = lp // ROW_TILE
    row = pl.BlockSpec((1, ROW_TILE, D_MODEL), lambda b, t: (b, t, 0))
    consts = (win, cw, cb, wout, lng, lnb)
    if attn is None:
        kern, args, specs = _ffn_kernel, (h,) + consts, [row]
    else:
        o, wo, lng0, lnb0 = attn
        kern = _attn_out_ffn_kernel
        args = (h, o, wo, lng0, lnb0) + consts
        specs = [row, row, _const_spec(wo.shape), _const_spec(lng0.shape),
                 _const_spec(lnb0.shape)]
    return pl.pallas_call(
        kern,
        grid=(bsz, nt),
        in_specs=specs + [_const_spec(a.shape) for a in consts],
        out_specs=row,
        out_shape=jax.ShapeDtypeStruct((bsz, lp, D_MODEL), F32),
        scratch_shapes=[pltpu.VMEM((HALO, 2 * D_FF), F32),
                        pltpu.VMEM((FF_BUFFERS, 2, ROW_TILE + HALO, FF_CHUNK), F32)],
        compiler_params=pltpu.CompilerParams(
            dimension_semantics=("arbitrary", "arbitrary"), vmem_limit_bytes=VMEM_LIMIT),
        name="conv_ffn" if attn is None else "attn_out_conv_ffn",
    )(*args)


def _mla_proj_kernel(h_ref, wkvd_ref, kvg_ref, wk_ref, wv_ref, wpe_ref, wper_ref, cosk_ref,
                     sink_ref, wqd_ref, qg_ref, wq_ref, wqr_ref, cosq_ref, sinq_ref,
                     q_ref, k_ref, v_ref):
    u = h_ref[0].astype(BF16)
    ckv = (_rms(_dot(u, wkvd_ref[...])) * kvg_ref[...]).astype(BF16)
    k_all = _dot(ckv, wk_ref[...])
    v_all = _dot(ckv, wv_ref[...])
    k_pe = _dot(u, wpe_ref[...]) * cosk_ref[...] + _dot(u, wper_ref[...]) * sink_ref[...]
    lane = lax.broadcasted_iota(jnp.int32, (1, HEAD_PAD), 1)
    ones_col = jnp.where(lane == V_HEAD, 1.0, 0.0)
    cq = (_rms(_dot(u, wqd_ref[...])) * qg_ref[...]).astype(BF16)
    q_all = _dot(cq, wq_ref[...])
    q_rot = _dot(cq, wqr_ref[...])
    cosq = cosq_ref[...]
    sinq = sinq_ref[...]
    for hd in range(MLA_HEADS):
        hl = slice(hd * HEAD_PAD, (hd + 1) * HEAD_PAD)
        k_ref[0, hd] = (k_all[:, hl] + k_pe).astype(BF16)
        v_ref[0, hd] = (v_all[:, hl] + ones_col).astype(BF16)
        q_ref[0, hd] = (q_all[:, hl] * cosq + q_rot[:, hl] * sinq).astype(BF16)


def _mla_proj(h, wkvd, kvg, wk, wv, wpe, wper, cosk, sink, wqd, qg, wq, wqr, cosq, sinq):
    bsz, lp, _ = h.shape
    nt = lp // ROW_TILE
    row = pl.BlockSpec((1, ROW_TILE, D_MODEL), lambda b, t: (b, t, 0))
    tab = pl.BlockSpec((ROW_TILE, HEAD_PAD), lambda b, t: (t, 0))
    head = pl.BlockSpec((1, MLA_HEADS, ROW_TILE, HEAD_PAD), lambda b, t: (b, 0, t, 0))
    shape = jax.ShapeDtypeStruct((bsz, MLA_HEADS, lp, HEAD_PAD), BF16)
    c = _const_spec
    return pl.pallas_call(
        _mla_proj_kernel,
        grid=(bsz, nt),
        in_specs=[row, c(wkvd.shape), c(kvg.shape), c(wk.shape), c(wv.shape), c(wpe.shape),
                  c(wper.shape), tab, tab, c(wqd.shape), c(qg.shape), c(wq.shape),
                  c(wqr.shape), tab, tab],
        out_specs=[head, head, head],
        out_shape=[shape, shape, shape],
        compiler_params=pltpu.CompilerParams(
            dimension_semantics=("arbitrary", "arbitrary"), vmem_limit_bytes=VMEM_LIMIT),
        name="mla_proj",
    )(h, wkvd, kvg, wk, wv, wpe, wper, cosk, sink, wqd, qg, wq, wqr, cosq, sinq)


def _attention_kernel(q_ref, k_ref, v_ref, o_ref):
    i = pl.program_id(2)
    rows = ROW_TILE
    r_i = lax.broadcasted_iota(jnp.int32, (rows, rows), 0)
    c_i = lax.broadcasted_iota(jnp.int32, (rows, rows), 1)
    n_rb = rows // Q_ROWS
    qs = [[q_ref[0, hh, rb * Q_ROWS:(rb + 1) * Q_ROWS, :] for rb in range(n_rb)]
          for hh in range(ATT_HEADS)]

    def step(j, carry, masked):
        k0 = pl.multiple_of(j * rows, rows)
        if masked:
            kpos = c_i + j * rows
            q_chunk = (r_i + i * rows) >> 6
            if masked == "diag":
                q_chunk = q_chunk - jnp.where(i > 0, 0, 1 << 20)
            ok = ((kpos >> 6) <= q_chunk) & (kpos >= PAD)
        scores = []
        for hh in range(ATT_HEADS):
            k_t = k_ref[0, hh, pl.ds(k0, rows), :]
            for rb in range(n_rb):
                s = lax.dot_general(qs[hh][rb], k_t, (((1,), (1,)), ((), ())),
                                    preferred_element_type=F32)
                if masked:
                    s = jnp.where(ok[rb * Q_ROWS:(rb + 1) * Q_ROWS], s, NEG_BIG)
                scores.append(s)
        out = []
        for hh in range(ATT_HEADS):
            v_t = v_ref[0, hh, pl.ds(k0, rows), :]
            for rb in range(n_rb):
                m, acc = carry[hh * n_rb + rb]
                s = scores[hh * n_rb + rb]
                m_new = jnp.maximum(m, jnp.max(s, axis=-1, keepdims=True))
                p = jnp.exp2(s - m_new).astype(BF16)
                out.append((m_new, jnp.exp2(m - m_new) * acc + _dot(p, v_t)))
        return tuple(out)

    carry = tuple((jnp.full((Q_ROWS, 1), NEG_BIG, F32), jnp.zeros((Q_ROWS, HEAD_PAD), F32))
                  for _ in range(ATT_HEADS * n_rb))
    carry = step(0, carry, "first")
    carry = lax.fori_loop(1, i, lambda j, c: step(j, c, None), carry)
    carry = step(i, carry, "diag")
    outs = []
    for hh in range(ATT_HEADS):
        accs = [carry[hh * n_rb + rb][1] for rb in range(n_rb)]
        outs.append(jnp.concatenate(
            [a[:, :V_HEAD] / a[:, V_HEAD:V_HEAD + 1] for a in accs], axis=0))
    o_ref[0] = jnp.concatenate(outs, axis=1).astype(BF16)


def _attention(q, k, v):
    bsz, nh, lp, _ = q.shape
    nt = lp // ROW_TILE
    hs = ATT_HEADS
    return pl.pallas_call(
        _attention_kernel,
        grid=(bsz, nh // hs, nt),
        in_specs=[pl.BlockSpec((1, hs, ROW_TILE, HEAD_PAD), lambda b, hp, i: (b, hp, i, 0)),
                  pl.BlockSpec((1, hs, lp, HEAD_PAD), lambda b, hp, i: (b, hp, 0, 0)),
                  pl.BlockSpec((1, hs, lp, HEAD_PAD), lambda b, hp, i: (b, hp, 0, 0))],
        out_specs=pl.BlockSpec((1, ROW_TILE, hs * V_HEAD), lambda b, hp, i: (b, i, hp)),
        out_shape=jax.ShapeDtypeStruct((bsz, lp, MLA_HEADS * V_HEAD), BF16),
        compiler_params=pltpu.CompilerParams(
            dimension_semantics=("arbitrary", "arbitrary", "arbitrary"),
            vmem_limit_bytes=VMEM_LIMIT),
        name="mla_attention",
    )(q, k, v)


def _row(v, width=None):
    v = v.astype(F32).reshape(1, -1)
    if width is not None and v.shape[1] < width:
        v = jnp.pad(v, ((0, 0), (0, width - v.shape[1])))
    return v


def _ffn_params(w_in, conv_w, conv_b, w_out):
    return w_in.astype(BF16), conv_w.astype(F32), _row(conv_b), w_out.astype(BF16)


def _rope_tables(lp):
    half = QK_ROPE // 2
    pos = jnp.maximum(jnp.arange(lp, dtype=F32) - PAD, 0.0)
    inv_freq = ROPE_THETA ** (-jnp.arange(half, dtype=F32) * 2.0 / QK_ROPE)
    ang = pos[:, None] * inv_freq[None, :]
    cos, sin = jnp.cos(ang), jnp.sin(ang)
    zeros = lambda n: jnp.zeros((lp, n), F32)
    tail = HEAD_PAD - QK_NOPE - QK_ROPE
    cosk = jnp.concatenate([zeros(QK_NOPE), cos, cos, zeros(tail)], axis=1)
    sink = jnp.concatenate([zeros(QK_NOPE), sin, sin, zeros(tail)], axis=1)
    c = (QK_NOPE + QK_ROPE) ** -0.5 * math.log2(math.e)
    cosq = c * jnp.concatenate([jnp.ones((lp, QK_NOPE), F32), cos, cos, zeros(tail)], axis=1)
    return cosk, sink, cosq, c * sink


def _rot_half_cols(w):
    half = QK_ROPE // 2
    return jnp.concatenate([-w[..., half:], w[..., :half]], axis=-1)


def _pad_last(a, before, total):
    pads = [(0, 0)] * (a.ndim - 1) + [(before, total - before - a.shape[-1])]
    return jnp.pad(a, pads)


def kernel(x, meta_tokens, m_w_in, m_conv_w, m_conv_b, m_dt_bias, m_a_log, m_d, m_norm_g, m_w_out, kv_w_down, kv_norm_g, kv_w_up, q_w_down, q_norm_g, q_w_up, o_w, f_w_in, f_conv_w, f_conv_b, f_w_out, ln_g, ln_b):
    bsz, seq, _ = x.shape
    lp = PAD + N_META + seq
    assert lp % ROW_TILE == 0 and (PAD + N_META) % CHUNK == 0
    meta = jnp.broadcast_to(meta_tokens[None].astype(x.dtype), (bsz, N_META, D_MODEL))
    h = jnp.concatenate([jnp.zeros((bsz, PAD, D_MODEL), x.dtype), meta, x], axis=1)

    w_in = m_w_in[0]
    wz = w_in[:, :SSM_D_INNER].astype(BF16)
    wxbc = w_in[:, SSM_D_INNER:SSM_D_INNER + SSM_CONV_DIM].astype(BF16)
    copies = HEAD_LANES // SSM_HEADS
    wdt = jnp.tile(w_in[:, SSM_D_INNER + SSM_CONV_DIM:], (1, copies)).astype(BF16)
    z, xbc, dt, da = _mamba_front(h, wz, wxbc, wdt, m_conv_w[0].astype(F32), _row(m_conv_b[0]),
                                  _row(jnp.tile(m_dt_bias[0], copies)),
                                  _row(jnp.tile(m_a_log[0], copies)))
    head_of_col = jnp.arange(SSM_D_INNER) // SSM_HEAD_DIM
    expand = (jnp.arange(HEAD_LANES)[:, None] % SSM_HEADS == head_of_col[None, :]).astype(BF16)
    h = _mamba_scan(h, z, xbc, dt, da, expand, _row(jnp.repeat(m_d[0], SSM_HEAD_DIM)),
                    _row(m_norm_g[0]), m_w_out[0].astype(BF16), _row(ln_g[0, 0]), _row(ln_b[0, 0]))
    h = _ffn(h, *_ffn_params(f_w_in[0], f_conv_w[0], f_conv_b[0], f_w_out[0]),
             _row(ln_g[0, 1]), _row(ln_b[0, 1]))

    cosk, sink, cosq, sinq = _rope_tables(lp)
    w_rope = kv_w_down[:, KV_LORA:]
    wpe = _pad_last(w_rope, QK_NOPE, HEAD_PAD).astype(BF16)
    wper = _pad_last(_rot_half_cols(w_rope), QK_NOPE, HEAD_PAD).astype(BF16)
    kv_up = kv_w_up.reshape(KV_LORA, MLA_HEADS, QK_NOPE + V_HEAD)
    wk = _pad_last(kv_up[..., :QK_NOPE], 0, HEAD_PAD).reshape(KV_LORA, -1).astype(BF16)
    wv = _pad_last(kv_up[..., QK_NOPE:], 0, HEAD_PAD).reshape(KV_LORA, -1).astype(BF16)
    q_up = q_w_up[0].reshape(Q_LORA, MLA_HEADS, QK_NOPE + QK_ROPE)
    wq = _pad_last(q_up, 0, HEAD_PAD).reshape(Q_LORA, -1).astype(BF16)
    wqr = _pad_last(_rot_half_cols(q_up[..., QK_NOPE:]), QK_NOPE, HEAD_PAD)
    wqr = wqr.reshape(Q_LORA, -1).astype(BF16)
    q, k, v = _mla_proj(h, kv_w_down[:, :KV_LORA].astype(BF16), _row(kv_norm_g), wk, wv, wpe,
                        wper, cosk, sink, q_w_down[0].astype(BF16), _row(q_norm_g[0]), wq, wqr,
                        cosq, sinq)
    o = _attention(q, k, v)
    h = _ffn(h, *_ffn_params(f_w_in[1], f_conv_w[1], f_conv_b[1], f_w_out[1]),
             _row(ln_g[1, 1]), _row(ln_b[1, 1]),
             attn=(o, o_w[0].astype(BF16), _row(ln_g[1, 0]), _row(ln_b[1, 0])))
    return h[:, PAD + N_META:]
```

```python
import functools
import math

import jax
import jax.numpy as jnp
from jax import lax
from jax.experimental import pallas as pl
from jax.experimental.pallas import tpu as pltpu

F32 = jnp.float32
BF16 = jnp.bfloat16

D_MODEL = 1024
N_META = 16
CHUNK = 64
DEPTH = 2
DN_ALPHA = (2.0 * DEPTH) ** 0.25
LN_EPS = 1e-5
RMS_EPS = 1e-6
LOG2_E = math.log2(math.e)

SSM_D_INNER = 2 * D_MODEL
SSM_HEAD_DIM = 64
SSM_HEADS = SSM_D_INNER // SSM_HEAD_DIM
SSM_GROUPS = 4
SSM_STATE = 128
SSM_CONV = 4
SSM_BC = SSM_GROUPS * SSM_STATE
SSM_CONV_DIM = SSM_D_INNER + 2 * SSM_BC
GROUP_COLS = SSM_D_INNER // SSM_GROUPS
HEAD_LANES = 128
FRONT_CHUNK = 512
FRONT_BUFFERS = 3

MLA_HEADS = 16
Q_LORA = 384
KV_LORA = 256
QK_NOPE = 64
QK_ROPE = 32
V_HEAD = 64
ROPE_THETA = 10000.0
HEAD_PAD = 128
ATT_HEADS = 8

D_FF = 2816
FFN_CONV = 3
FF_CHUNK = 256
N_FF_CHUNKS = D_FF // FF_CHUNK
FF_AHEAD = 2
SUBLANES = 8
LANES = 128

PAD = 112
ROW_TILE = 384
HALO = 8
NEG_BIG = -1e30
VMEM_LIMIT = 56 * 1024 * 1024


def _const_spec(shape):
    nd = len(shape)
    return pl.BlockSpec(shape, lambda *_: (0,) * nd, pipeline_mode=pl.Buffered(1))


def _layer_norm(v, g, b):
    mu = jnp.mean(v, axis=-1, keepdims=True)
    d = v - mu
    var = jnp.mean(d * d, axis=-1, keepdims=True)
    return d * lax.rsqrt(var + LN_EPS) * g + b


def _rms(v):
    return v * lax.rsqrt(jnp.mean(v * v, axis=-1, keepdims=True) + RMS_EPS)


def _silu(v):
    return v / (1.0 + jnp.exp(-v))


def _split_bf16(v, parts):
    out = []
    rem = v
    for _ in range(parts):
        p = rem.astype(BF16)
        out.append(p)
        rem = rem - p.astype(F32)
    return out


def _dot(a, b):
    return jnp.dot(a, b, preferred_element_type=F32)


def _dot_split(lhs_parts, rhs):
    acc = _dot(lhs_parts[0], rhs)
    for p in lhs_parts[1:]:
        acc = acc + _dot(p, rhs)
    return acc


def _mamba_front_kernel(h_ref, wz_ref, wxbc_ref, wdt_ref, cw_ref, cb_ref, dtb_ref, alog_ref,
                        z_ref, xbc_ref, dt_ref, da_ref, halo_ref, buf_ref):
    t = pl.program_id(1)
    rows = ROW_TILE
    width = FRONT_CHUNK
    n_x = SSM_CONV_DIM // width
    n_z = SSM_D_INNER // width
    u = h_ref[0].astype(BF16)

    @pl.when(t == 0)
    def _():
        halo_ref[...] = jnp.zeros(halo_ref.shape, F32)

    def project(c):
        cs = slice(c * width, (c + 1) * width)
        slot = c % FRONT_BUFFERS
        buf_ref[slot, 0:HALO, :] = halo_ref[:, cs]
        pre = _dot(u, wxbc_ref[:, cs])
        buf_ref[slot, HALO:HALO + rows, :] = pre
        buf_ref[slot, HALO:HALO + PAD, :] = jnp.where(t > 0, pre[:PAD], 0.0)

    def gate(c):
        cs = slice(c * width, (c + 1) * width)
        z_ref[0, :, cs] = _dot(u, wz_ref[:, cs]).astype(BF16)

    for c in range(FRONT_BUFFERS - 1):
        project(c)
    for c in range(n_x):
        if c + FRONT_BUFFERS - 1 < n_x:
            project(c + FRONT_BUFFERS - 1)
        if c < n_z:
            gate(c)
        cs = slice(c * width, (c + 1) * width)
        slot = c % FRONT_BUFFERS
        acc = cb_ref[:, cs]
        for k in range(SSM_CONV):
            off = HALO - (SSM_CONV - 1) + k
            acc = acc + cw_ref[k:k + 1, cs] * buf_ref[slot, off:off + rows, :]
        halo_ref[:, cs] = buf_ref[slot, rows:rows + HALO, :]
        act = _silu(acc).astype(BF16)
        xbc_ref[0, :, cs] = act
        xbc_ref[0, 0:PAD, cs] = jnp.where(t > 0, act[:PAD], jnp.zeros((), BF16))

    x = _dot(u, wdt_ref[...]) + dtb_ref[...]
    dt = jnp.maximum(x, 0.0) + jnp.log1p(jnp.exp(-jnp.abs(x)))
    da = dt * (-jnp.exp(alog_ref[...]))
    dt_ref[0] = dt
    da_ref[0] = da
    dt_ref[0, 0:PAD, :] = jnp.where(t > 0, dt[:PAD], 0.0)
    da_ref[0, 0:PAD, :] = jnp.where(t > 0, da[:PAD], 0.0)


def _mamba_front(h, wz, wxbc, wdt, cw, cb, dtb, alog):
    bsz, lp, _ = h.shape
    nt = lp // ROW_TILE
    row = lambda w: pl.BlockSpec((1, ROW_TILE, w), lambda b, t: (b, t, 0))
    return pl.pallas_call(
        _mamba_front_kernel,
        grid=(bsz, nt),
        in_specs=[row(D_MODEL), _const_spec(wz.shape), _const_spec(wxbc.shape),
                  _const_spec(wdt.shape), _const_spec(cw.shape), _const_spec(cb.shape),
                  _const_spec(dtb.shape), _const_spec(alog.shape)],
        out_specs=[row(SSM_D_INNER), row(SSM_CONV_DIM), row(HEAD_LANES), row(HEAD_LANES)],
        out_shape=[jax.ShapeDtypeStruct((bsz, lp, SSM_D_INNER), BF16),
                   jax.ShapeDtypeStruct((bsz, lp, SSM_CONV_DIM), BF16),
                   jax.ShapeDtypeStruct((bsz, lp, HEAD_LANES), F32),
                   jax.ShapeDtypeStruct((bsz, lp, HEAD_LANES), F32)],
        scratch_shapes=[pltpu.VMEM((HALO, SSM_CONV_DIM), F32),
                        pltpu.VMEM((FRONT_BUFFERS, ROW_TILE + HALO, FRONT_CHUNK), F32)],
        compiler_params=pltpu.CompilerParams(
            dimension_semantics=("arbitrary", "arbitrary"), vmem_limit_bytes=VMEM_LIMIT),
        name="mamba_front",
    )(h, wz, wxbc, wdt, cw, cb, dtb, alog)


def _mamba_scan_kernel(h_ref, z_ref, xs_ref, b_ref, c_ref, dt_ref, da_ref, e_ref, dskip_ref,
                       ng_ref, wout_ref, lng_ref, lnb_ref, out_ref,
                       acs_ref, dte_ref, y_ref, state_ref):
    t = pl.program_id(1)
    rows = ROW_TILE
    n_chunks = rows // CHUNK

    @pl.when(t == 0)
    def _():
        state_ref[...] = jnp.zeros(state_ref.shape, F32)

    r_i = lax.broadcasted_iota(jnp.int32, (rows, rows), 0)
    c_i = lax.broadcasted_iota(jnp.int32, (rows, rows), 1)
    tri = jnp.where(((r_i >> 6) == (c_i >> 6)) & (c_i <= r_i), 1.0, 0.0).astype(BF16)
    acs = sum(_dot(tri, p) for p in _split_bf16(da_ref[0], 3))
    lane_copy = lax.broadcasted_iota(jnp.int32, (rows, HEAD_LANES), 1) // SSM_HEADS

    def packed_terms(v, parts):
        terms = _split_bf16(v, parts)
        out = jnp.zeros((rows, HEAD_LANES), BF16)
        for k, term in enumerate(terms):
            out = jnp.where(lane_copy == k, term, out)
        return out

    acs_ref[...] = _dot(packed_terms(acs, 3), e_ref[...]) * LOG2_E
    dte_ref[...] = _dot(packed_terms(dt_ref[0], 2), e_ref[...])

    s_i = lax.broadcasted_iota(jnp.int32, (CHUNK, 2 * CHUNK), 0)
    l_i = lax.broadcasted_iota(jnp.int32, (CHUNK, 2 * CHUNK), 1)
    diag = (l_i & (CHUNK - 1)) == s_i
    causal = (l_i & (CHUNK - 1)) <= s_i
    br_i = lax.broadcasted_iota(jnp.int32, (2 * CHUNK, 2 * CHUNK), 0)
    bc_i = lax.broadcasted_iota(jnp.int32, (2 * CHUNK, 2 * CHUNK), 1)
    blockdiag = (br_i >> 6) == (bc_i >> 6)

    def chunk_body(c, carry):
        r0 = pl.multiple_of(c * CHUNK, CHUNK)
        acs_c = acs_ref[pl.ds(r0, CHUNK), :]
        dte_c = dte_ref[pl.ds(r0, CHUNK), :]
        xs_c = xs_ref[0, pl.ds(r0, CHUNK), :]
        b_c = b_ref[0, pl.ds(r0, CHUNK), :]
        c_c = c_ref[0, pl.ds(r0, CHUNK), :]
        last = acs_c[CHUNK - 1:CHUNK, :]
        groups = [slice(g * SSM_STATE, (g + 1) * SSM_STATE) for g in range(SSM_GROUPS)]
        gcols = [slice(g * GROUP_COLS, (g + 1) * GROUP_COLS) for g in range(SSM_GROUPS)]
        pair = 2 * SSM_HEAD_DIM
        states = [state_ref[g] for g in range(SSM_GROUPS)]
        cb2 = [lax.dot_general(c_c[:, gl], jnp.concatenate([b_c[:, gl], b_c[:, gl]], axis=0),
                               (((1,), (1,)), ((), ())), preferred_element_type=F32)
               for gl in groups]
        y_off = [_dot(c_c[:, gl], st.astype(BF16)) for gl, st in zip(groups, states)]
        x_dt = xs_c.astype(F32) * dte_c
        xw = (x_dt * jnp.exp2(last - acs_c)).astype(BF16)
        x_dt = x_dt.astype(BF16)
        s_new = [lax.dot_general(b_c[:, gl], xw[:, gc], (((0,), (0,)), ((), ())),
                                 preferred_element_type=F32) for gl, gc in zip(groups, gcols)]
        grow = jnp.exp2(acs_c)
        chunk_decay = jnp.exp2(last)
        for g in range(SSM_GROUPS):
            for jj in range(GROUP_COLS // pair):
                pl_ = slice(g * GROUP_COLS + jj * pair, g * GROUP_COLS + (jj + 1) * pair)
                a_p = acs_c[:, pl_]
                a_row = jnp.sum(jnp.where(diag, a_p, 0.0), axis=0, keepdims=True)
                decay = jnp.exp2(jnp.where(causal, a_p - a_row, NEG_BIG))
                m2 = (cb2[g] * decay).astype(BF16)
                x_p = x_dt[:, pl_]
                x_bd = jnp.where(blockdiag, jnp.concatenate([x_p, x_p], axis=0),
                                 jnp.zeros((), BF16))
                y_ref[pl.ds(r0, CHUNK), pl_] = (
                    _dot(m2, x_bd) + y_off[g][:, jj * pair:(jj + 1) * pair] * grow[:, pl_])
            state_ref[g] = states[g] * chunk_decay[:, gcols[g]] + s_new[g]
        return carry

    lax.fori_loop(0, n_chunks, chunk_body, 0)

    y = y_ref[...] + xs_ref[0].astype(F32) * dskip_ref[...]
    yg = y * _silu(z_ref[0].astype(F32))
    parts = [_rms(yg[:, g * GROUP_COLS:(g + 1) * GROUP_COLS]) for g in range(SSM_GROUPS)]
    yn = jnp.concatenate(parts, axis=1) * ng_ref[...]
    mix = _dot(yn.astype(BF16), wout_ref[...])
    out_ref[0] = _layer_norm(DN_ALPHA * h_ref[0] + mix, lng_ref[...], lnb_ref[...])


def _mamba_scan(h, z, xbc, dt, da, e, dskip, ng, wout, lng, lnb):
    bsz, lp, _ = h.shape
    nt = lp // ROW_TILE
    row = lambda w, cb=0: pl.BlockSpec((1, ROW_TILE, w), lambda b, t: (b, t, cb))
    return pl.pallas_call(
        _mamba_scan_kernel,
        grid=(bsz, nt),
        in_specs=[row(D_MODEL), row(SSM_D_INNER), row(SSM_D_INNER),
                  row(SSM_BC, SSM_D_INNER // SSM_BC), row(SSM_BC, SSM_D_INNER // SSM_BC + 1),
                  row(HEAD_LANES), row(HEAD_LANES),
                  _const_spec(e.shape), _const_spec(dskip.shape), _const_spec(ng.shape),
                  _const_spec(wout.shape), _const_spec(lng.shape), _const_spec(lnb.shape)],
        out_specs=row(D_MODEL),
        out_shape=jax.ShapeDtypeStruct((bsz, lp, D_MODEL), F32),
        scratch_shapes=[pltpu.VMEM((ROW_TILE, SSM_D_INNER), F32),
                        pltpu.VMEM((ROW_TILE, SSM_D_INNER), F32),
                        pltpu.VMEM((ROW_TILE, SSM_D_INNER), F32),
                        pltpu.VMEM((SSM_GROUPS, SSM_STATE, GROUP_COLS), F32)],
        compiler_params=pltpu.CompilerParams(
            dimension_semantics=("arbitrary", "arbitrary"), vmem_limit_bytes=VMEM_LIMIT),
        name="mamba_scan",
    )(h, z, xbc, xbc, xbc, dt, da, e, dskip, ng, wout, lng, lnb)


def _ffn_body(h, t, win_ref, cw_ref, cb_ref, wout_ref, lng_ref, lnb_ref, out_ref,
              halo_ref, perm_ref):
    rows = ROW_TILE
    stride = rows // SUBLANES
    n_slabs = D_MODEL // LANES

    @pl.when(t == 0)
    def _():
        halo_ref[...] = jnp.zeros(halo_ref.shape, F32)

    for s in range(n_slabs):
        perm_ref[s] = h[:, s * LANES:(s + 1) * LANES]
    h = jnp.concatenate(
        [jnp.concatenate([perm_ref[s, pl.ds(a, SUBLANES, stride=stride), :]
                          for s in range(n_slabs)], axis=1) for a in range(stride)], axis=0)

    p_i = lax.broadcasted_iota(jnp.int32, (rows, LANES), 0)
    keep = (p_i & (SUBLANES - 1)) * stride + (p_i >> 3) >= jnp.where(t > 0, 0, PAD)
    u = jnp.concatenate([jnp.where(keep, h[:, s * LANES:(s + 1) * LANES], 0.0)
                         for s in range(n_slabs)], axis=1).astype(BF16)
    first_sublane = lax.broadcasted_iota(jnp.int32, (SUBLANES, FF_CHUNK), 0) == 0

    def cols(cb):
        return (slice(cb * FF_CHUNK, (cb + 1) * FF_CHUNK),
                slice(D_FF + cb * FF_CHUNK, D_FF + (cb + 1) * FF_CHUNK))

    def conv(x, cs):
        tail = halo_ref[:, cs]
        halo_ref[:, cs] = x[rows - 2 * SUBLANES:, :]

        def wrapped(cur, prev):
            return jnp.where(first_sublane, pltpu.roll(prev, 1, 0), pltpu.roll(cur, 1, 0))

        back1 = wrapped(x[rows - SUBLANES:, :], tail[SUBLANES:, :])
        back2 = wrapped(x[rows - 2 * SUBLANES:rows - SUBLANES, :], tail[:SUBLANES, :])
        x1 = jnp.concatenate([back1, x[:rows - SUBLANES, :]], axis=0)
        x2 = jnp.concatenate([back2, back1, x[:rows - 2 * SUBLANES, :]], axis=0)
        return (cb_ref[:, cs] + cw_ref[2:3, cs] * x + cw_ref[1:2, cs] * x1
                + cw_ref[0:1, cs] * x2)

    def project(cb):
        return [_dot(u, win_ref[:, cs]) for cs in cols(cb)]

    pre = {cb: project(cb) for cb in range(FF_AHEAD)}
    acc = jnp.zeros((rows, D_MODEL), F32)
    for cb in range(N_FF_CHUNKS):
        if cb + FF_AHEAD < N_FF_CHUNKS:
            pre[cb + FF_AHEAD] = project(cb + FF_AHEAD)
        gate, up = pre.pop(cb)
        gate_cols, up_cols = cols(cb)
        act = (_silu(conv(gate, gate_cols)) * conv(up, up_cols)).astype(BF16)
        acc = acc + _dot(act, wout_ref[cb * FF_CHUNK:(cb + 1) * FF_CHUNK, :])
    out = _layer_norm(DN_ALPHA * h + acc, lng_ref[...], lnb_ref[...])
    for a in range(stride):
        for s in range(n_slabs):
            perm_ref[s, pl.ds(a, SUBLANES, stride=stride), :] = (
                out[a * SUBLANES:(a + 1) * SUBLANES, s * LANES:(s + 1) * LANES])
    out_ref[0] = jnp.concatenate([perm_ref[s] for s in range(n_slabs)], axis=1)


def _ffn_kernel(h_ref, win_ref, cw_ref, cb_ref, wout_ref, lng_ref, lnb_ref, out_ref,
                halo_ref, perm_ref):
    _ffn_body(h_ref[0], pl.program_id(1), win_ref, cw_ref, cb_ref, wout_ref, lng_ref, lnb_ref,
              out_ref, halo_ref, perm_ref)


def _attn_out_ffn_kernel(h_ref, o_ref, wo_ref, lng0_ref, lnb0_ref, win_ref, cw_ref, cb_ref,
                         wout_ref, lng_ref, lnb_ref, out_ref, halo_ref, perm_ref):
    mix = _dot(o_ref[0], wo_ref[...])
    h = _layer_norm(DN_ALPHA * h_ref[0] + mix, lng0_ref[...], lnb0_ref[...])
    _ffn_body(h, pl.program_id(1), win_ref, cw_ref, cb_ref, wout_ref, lng_ref, lnb_ref,
              out_ref, halo_ref, perm_ref)


def _ffn(h, win, cw, cb, wout, lng, lnb, attn=None):
    bsz, lp, _ = h.shape
    nt = lp // ROW_TILE
    row = pl.BlockSpec((1, ROW_TILE, D_MODEL), lambda b, t: (b, t, 0))
    consts = (win, cw, cb, wout, lng, lnb)
    if attn is None:
        kern, args, specs = _ffn_kernel, (h,) + consts, [row]
    else:
        o, wo, lng0, lnb0 = attn
        kern = _attn_out_ffn_kernel
        args = (h, o, wo, lng0, lnb0) + consts
        specs = [row, row, _const_spec(wo.shape), _const_spec(lng0.shape),
                 _const_spec(lnb0.shape)]
    return pl.pallas_call(
        kern,
        grid=(bsz, nt),
        in_specs=specs + [_const_spec(a.shape) for a in consts],
        out_specs=row,
        out_shape=jax.ShapeDtypeStruct((bsz, lp, D_MODEL), F32),
        scratch_shapes=[pltpu.VMEM((2 * SUBLANES, 2 * D_FF), F32),
                        pltpu.VMEM((D_MODEL // LANES, ROW_TILE, LANES), F32)],
        compiler_params=pltpu.CompilerParams(
            dimension_semantics=("arbitrary", "arbitrary"), vmem_limit_bytes=VMEM_LIMIT),
        name="conv_ffn" if attn is None else "attn_out_conv_ffn",
    )(*args)


def _mla_proj_kernel(h_ref, wkvd_ref, kvg_ref, wk_ref, wv_ref, wpe_ref, wper_ref, cosk_ref,
                     sink_ref, wqd_ref, qg_ref, wq_ref, wqr_ref, cosq_ref, sinq_ref,
                     q_ref, k_ref, v_ref):
    u = h_ref[0].astype(BF16)
    ckv = (_rms(_dot(u, wkvd_ref[...])) * kvg_ref[...]).astype(BF16)
    k_all = _dot(ckv, wk_ref[...])
    v_all = _dot(ckv, wv_ref[...])
    k_pe = _dot(u, wpe_ref[...]) * cosk_ref[...] + _dot(u, wper_ref[...]) * sink_ref[...]
    lane = lax.broadcasted_iota(jnp.int32, (1, HEAD_PAD), 1)
    ones_col = jnp.where(lane == V_HEAD, 1.0, 0.0)
    cq = (_rms(_dot(u, wqd_ref[...])) * qg_ref[...]).astype(BF16)
    q_all = _dot(cq, wq_ref[...])
    q_rot = _dot(cq, wqr_ref[...])
    cosq = cosq_ref[...]
    sinq = sinq_ref[...]
    for hd in range(MLA_HEADS):
        hl = slice(hd * HEAD_PAD, (hd + 1) * HEAD_PAD)
        k_ref[0, hd] = (k_all[:, hl] + k_pe).astype(BF16)
        v_ref[0, hd] = (v_all[:, hl] + ones_col).astype(BF16)
        q_ref[0, hd] = (q_all[:, hl] * cosq + q_rot[:, hl] * sinq).astype(BF16)


def _mla_proj(h, wkvd, kvg, wk, wv, wpe, wper, cosk, sink, wqd, qg, wq, wqr, cosq, sinq):
    bsz, lp, _ = h.shape
    nt = lp // ROW_TILE
    row = pl.BlockSpec((1, ROW_TILE, D_MODEL), lambda b, t: (b, t, 0))
    tab = pl.BlockSpec((ROW_TILE, HEAD_PAD), lambda b, t: (t, 0))
    head = pl.BlockSpec((1, MLA_HEADS, ROW_TILE, HEAD_PAD), lambda b, t: (b, 0, t, 0))
    shape = jax.ShapeDtypeStruct((bsz, MLA_HEADS, lp, HEAD_PAD), BF16)
    c = _const_spec
    return pl.pallas_call(
        _mla_proj_kernel,
        grid=(bsz, nt),
        in_specs=[row, c(wkvd.shape), c(kvg.shape), c(wk.shape), c(wv.shape), c(wpe.shape),
                  c(wper.shape), tab, tab, c(wqd.shape), c(qg.shape), c(wq.shape),
                  c(wqr.shape), tab, tab],
        out_specs=[head, head, head],
        out_shape=[shape, shape, shape],
        compiler_params=pltpu.CompilerParams(
            dimension_semantics=("arbitrary", "arbitrary"), vmem_limit_bytes=VMEM_LIMIT),
        name="mla_proj",
    )(h, wkvd, kvg, wk, wv, wpe, wper, cosk, sink, wqd, qg, wq, wqr, cosq, sinq)


def _attention_kernel(q_ref, k_ref, v_ref, o_ref):
    i = pl.program_id(2)
    rows = ROW_TILE
    r_i = lax.broadcasted_iota(jnp.int32, (rows, rows), 0)
    c_i = lax.broadcasted_iota(jnp.int32, (rows, rows), 1)
    qs = [q_ref[0, hh] for hh in range(ATT_HEADS)]

    def visible(j, kind):
        if kind is None:
            return None
        kpos = c_i + j * rows
        q_chunk = (r_i + i * rows) >> 6
        if kind == "diag":
            q_chunk = q_chunk - jnp.where(i > 0, 0, 1 << 20)
        return ((kpos >> 6) <= q_chunk) & (kpos >= PAD)

    def scores(j, hh, ok):
        k0 = pl.multiple_of(j * rows, rows)
        s = lax.dot_general(qs[hh], k_ref[0, hh, pl.ds(k0, rows), :],
                            (((1,), (1,)), ((), ())), preferred_element_type=F32)
        return s if ok is None else jnp.where(ok, s, NEG_BIG)

    def absorb(j, hh, s, state):
        k0 = pl.multiple_of(j * rows, rows)
        m, acc = state
        m_new = jnp.maximum(m, jnp.max(s, axis=-1, keepdims=True))
        p = jnp.exp2(s - m_new).astype(BF16)
        return m_new, jnp.exp2(m - m_new) * acc + _dot(p, v_ref[0, hh, pl.ds(k0, rows), :])

    def steps(tiles, carry):
        carry = list(carry)
        masks = [visible(j, kind) for j, kind in tiles]
        cur = [scores(tiles[0][0], hh, masks[0]) for hh in range(ATT_HEADS)]
        for n, (j, _) in enumerate(tiles):
            nxt = []
            for hh in range(ATT_HEADS):
                if n + 1 < len(tiles):
                    nxt.append(scores(tiles[n + 1][0], hh, masks[n + 1]))
                carry[hh] = absorb(j, hh, cur[hh], carry[hh])
            cur = nxt
        return tuple(carry)

    carry = tuple((jnp.full((rows, 1), NEG_BIG, F32), jnp.zeros((rows, HEAD_PAD), F32))
                  for _ in range(ATT_HEADS))
    n_mid = jnp.maximum(i - 1, 0)
    odd = n_mid & 1
    carry = lax.fori_loop(0, odd, lambda _, c: steps([(1, None)], c), carry)
    carry = lax.fori_loop(
        0, n_mid >> 1,
        lambda p, c: steps([(1 + odd + 2 * p, None), (2 + odd + 2 * p, None)], c), carry)
    carry = steps([(0, "first"), (i, "diag")], carry)
    outs = [acc[:, :V_HEAD] / acc[:, V_HEAD:V_HEAD + 1] for _, acc in carry]
    o_ref[0] = jnp.concatenate(outs, axis=1).astype(BF16)


def _attention(q, k, v):
    bsz, nh, lp, _ = q.shape
    nt = lp // ROW_TILE
    hs = ATT_HEADS
    return pl.pallas_call(
        _attention_kernel,
        grid=(bsz, nh // hs, nt),
        in_specs=[pl.BlockSpec((1, hs, ROW_TILE, HEAD_PAD), lambda b, hp, i: (b, hp, i, 0)),
                  pl.BlockSpec((1, hs, lp, HEAD_PAD), lambda b, hp, i: (b, hp, 0, 0)),
                  pl.BlockSpec((1, hs, lp, HEAD_PAD), lambda b, hp, i: (b, hp, 0, 0))],
        out_specs=pl.BlockSpec((1, ROW_TILE, hs * V_HEAD), lambda b, hp, i: (b, i, hp)),
        out_shape=jax.ShapeDtypeStruct((bsz, lp, MLA_HEADS * V_HEAD), BF16),
        compiler_params=pltpu.CompilerParams(
            dimension_semantics=("arbitrary", "arbitrary", "arbitrary"),
            vmem_limit_bytes=VMEM_LIMIT),
        name="mla_attention",
    )(q, k, v)


def _row(v, width=None):
    v = v.astype(F32).reshape(1, -1)
    if width is not None and v.shape[1] < width:
        v = jnp.pad(v, ((0, 0), (0, width - v.shape[1])))
    return v


def _ffn_params(w_in, conv_w, conv_b, w_out):
    return w_in.astype(BF16), conv_w.astype(F32), _row(conv_b), w_out.astype(BF16)


def _rope_tables(lp):
    half = QK_ROPE // 2
    pos = jnp.maximum(jnp.arange(lp, dtype=F32) - PAD, 0.0)
    inv_freq = ROPE_THETA ** (-jnp.arange(half, dtype=F32) * 2.0 / QK_ROPE)
    ang = pos[:, None] * inv_freq[None, :]
    cos, sin = jnp.cos(ang), jnp.sin(ang)
    zeros = lambda n: jnp.zeros((lp, n), F32)
    tail = HEAD_PAD - QK_NOPE - QK_ROPE
    cosk = jnp.concatenate([zeros(QK_NOPE), cos, cos, zeros(tail)], axis=1)
    sink = jnp.concatenate([zeros(QK_NOPE), sin, sin, zeros(tail)], axis=1)
    c = (QK_NOPE + QK_ROPE) ** -0.5 * math.log2(math.e)
    cosq = c * jnp.concatenate([jnp.ones((lp, QK_NOPE), F32), cos, cos, zeros(tail)], axis=1)
    return cosk, sink, cosq, c * sink


def _rot_half_cols(w):
    half = QK_ROPE // 2
    return jnp.concatenate([-w[..., half:], w[..., :half]], axis=-1)


def _pad_last(a, before, total):
    pads = [(0, 0)] * (a.ndim - 1) + [(before, total - before - a.shape[-1])]
    return jnp.pad(a, pads)


def kernel(x, meta_tokens, m_w_in, m_conv_w, m_conv_b, m_dt_bias, m_a_log, m_d, m_norm_g, m_w_out, kv_w_down, kv_norm_g, kv_w_up, q_w_down, q_norm_g, q_w_up, o_w, f_w_in, f_conv_w, f_conv_b, f_w_out, ln_g, ln_b):
    bsz, seq, _ = x.shape
    lp = PAD + N_META + seq
    assert lp % ROW_TILE == 0 and (PAD + N_META) % CHUNK == 0
    meta = jnp.broadcast_to(meta_tokens[None].astype(x.dtype), (bsz, N_META, D_MODEL))
    h = jnp.concatenate([jnp.zeros((bsz, PAD, D_MODEL), x.dtype), meta, x], axis=1)

    w_in = m_w_in[0]
    wz = w_in[:, :SSM_D_INNER].astype(BF16)
    wxbc = w_in[:, SSM_D_INNER:SSM_D_INNER + SSM_CONV_DIM].astype(BF16)
    copies = HEAD_LANES // SSM_HEADS
    wdt = jnp.tile(w_in[:, SSM_D_INNER + SSM_CONV_DIM:], (1, copies)).astype(BF16)
    z, xbc, dt, da = _mamba_front(h, wz, wxbc, wdt, m_conv_w[0].astype(F32), _row(m_conv_b[0]),
                                  _row(jnp.tile(m_dt_bias[0], copies)),
                                  _row(jnp.tile(m_a_log[0], copies)))
    head_of_col = jnp.arange(SSM_D_INNER) // SSM_HEAD_DIM
    expand = (jnp.arange(HEAD_LANES)[:, None] % SSM_HEADS == head_of_col[None, :]).astype(BF16)
    h = _mamba_scan(h, z, xbc, dt, da, expand, _row(jnp.repeat(m_d[0], SSM_HEAD_DIM)),
                    _row(m_norm_g[0]), m_w_out[0].astype(BF16), _row(ln_g[0, 0]), _row(ln_b[0, 0]))
    h = _ffn(h, *_ffn_params(f_w_in[0], f_conv_w[0], f_conv_b[0], f_w_out[0]),
             _row(ln_g[0, 1]), _row(ln_b[0, 1]))

    cosk, sink, cosq, sinq = _rope_tables(lp)
    w_rope = kv_w_down[:, KV_LORA:]
    wpe = _pad_last(w_rope, QK_NOPE, HEAD_PAD).astype(BF16)
    wper = _pad_last(_rot_half_cols(w_rope), QK_NOPE, HEAD_PAD).astype(BF16)
    kv_up = kv_w_up.reshape(KV_LORA, MLA_HEADS, QK_NOPE + V_HEAD)
    wk = _pad_last(kv_up[..., :QK_NOPE], 0, HEAD_PAD).reshape(KV_LORA, -1).astype(BF16)
    wv = _pad_last(kv_up[..., QK_NOPE:], 0, HEAD_PAD).reshape(KV_LORA, -1).astype(BF16)
    q_up = q_w_up[0].reshape(Q_LORA, MLA_HEADS, QK_NOPE + QK_ROPE)
    wq = _pad_last(q_up, 0, HEAD_PAD).reshape(Q_LORA, -1).astype(BF16)
    wqr = _pad_last(_rot_half_cols(q_up[..., QK_NOPE:]), QK_NOPE, HEAD_PAD)
    wqr = wqr.reshape(Q_LORA, -1).astype(BF16)
    q, k, v = _mla_proj(h, kv_w_down[:, :KV_LORA].astype(BF16), _row(kv_norm_g), wk, wv, wpe,
                        wper, cosk, sink, q_w_down[0].astype(BF16), _row(q_norm_g[0]), wq, wqr,
                        cosq, sinq)
    o = _attention(q, k, v)
    h = _ffn(h, *_ffn_params(f_w_in[1], f_conv_w[1], f_conv_b[1], f_w_out[1]),
             _row(ln_g[1, 1]), _row(ln_b[1, 1]),
             attn=(o, o_w[0].astype(BF16), _row(ln_g[1, 0]), _row(ln_b[1, 0])))
    return h[:, PAD + N_META:]
```

```python
import functools
import math

import jax
import jax.numpy as jnp
from jax import lax
from jax.experimental import pallas as pl
from jax.experimental.pallas import tpu as pltpu

F32 = jnp.float32
BF16 = jnp.bfloat16

D_MODEL = 1024
N_META = 16
CHUNK = 64
DEPTH = 2
DN_ALPHA = (2.0 * DEPTH) ** 0.25
LN_EPS = 1e-5
RMS_EPS = 1e-6
LOG2_E = math.log2(math.e)

SSM_D_INNER = 2 * D_MODEL
SSM_HEAD_DIM = 64
SSM_HEADS = SSM_D_INNER // SSM_HEAD_DIM
SSM_GROUPS = 4
SSM_STATE = 128
SSM_CONV = 4
SSM_BC = SSM_GROUPS * SSM_STATE
SSM_CONV_DIM = SSM_D_INNER + 2 * SSM_BC
GROUP_COLS = SSM_D_INNER // SSM_GROUPS
HEAD_LANES = 128
FRONT_CHUNK = 512
FRONT_BUFFERS = 3

MLA_HEADS = 16
Q_LORA = 384
KV_LORA = 256
QK_NOPE = 64
QK_ROPE = 32
V_HEAD = 64
ROPE_THETA = 10000.0
HEAD_PAD = 128
ATT_HEADS = 8

D_FF = 2816
FFN_CONV = 3
FF_CHUNK = 256
N_FF_CHUNKS = D_FF // FF_CHUNK
FF_AHEAD = 2
SUBLANES = 8
LANES = 128

PAD = 112
PREFIX = PAD + N_META
ROW_TILE = 384
HALO = 8
NEG_BIG = -1e30
VMEM_LIMIT = 60 * 1024 * 1024


def _const_spec(shape):
    nd = len(shape)
    return pl.BlockSpec(shape, lambda *_: (0,) * nd, pipeline_mode=pl.Buffered(1))


def _layer_norm(v, g, b):
    mu = jnp.mean(v, axis=-1, keepdims=True)
    d = v - mu
    var = jnp.mean(d * d, axis=-1, keepdims=True)
    return d * lax.rsqrt(var + LN_EPS) * g + b


def _rms(v):
    return v * lax.rsqrt(jnp.mean(v * v, axis=-1, keepdims=True) + RMS_EPS)


def _silu(v):
    return v / (1.0 + jnp.exp(-v))


def _split_bf16(v, parts):
    out = []
    rem = v
    for _ in range(parts):
        p = rem.astype(BF16)
        out.append(p)
        rem = rem - p.astype(F32)
    return out


def _dot(a, b):
    return jnp.dot(a, b, preferred_element_type=F32)


def _dot_split(lhs_parts, rhs):
    acc = _dot(lhs_parts[0], rhs)
    for p in lhs_parts[1:]:
        acc = acc + _dot(p, rhs)
    return acc


def _sublane_major(perm_ref, v):
    rows, width = v.shape
    stride = rows // SUBLANES
    n = width // LANES
    for s in range(n):
        perm_ref[s] = v[:, s * LANES:(s + 1) * LANES]
    return jnp.concatenate(
        [jnp.concatenate([perm_ref[s, pl.ds(a, SUBLANES, stride=stride), :]
                          for s in range(n)], axis=1) for a in range(stride)], axis=0)


def _natural_order(perm_ref, v):
    rows, width = v.shape
    stride = rows // SUBLANES
    n = width // LANES
    for a in range(stride):
        for s in range(n):
            perm_ref[s, pl.ds(a, SUBLANES, stride=stride), :] = (
                v[a * SUBLANES:(a + 1) * SUBLANES, s * LANES:(s + 1) * LANES])
    return jnp.concatenate([perm_ref[s] for s in range(n)], axis=1)


def _matmul_rows(h, t):
    rows, width = h.shape
    p_i = lax.broadcasted_iota(jnp.int32, (rows, LANES), 0)
    time = (p_i & (SUBLANES - 1)) * (rows // SUBLANES) + (p_i >> 3)
    keep = time >= jnp.where(t > 0, 0, PAD)
    return jnp.concatenate([jnp.where(keep, h[:, s * LANES:(s + 1) * LANES], 0.0)
                            for s in range(width // LANES)], axis=1).astype(BF16)


def _causal_conv(x, halo_ref, cw_ref, cb_ref, cs):
    rows, width = x.shape
    taps = cw_ref.shape[0]
    first_sublane = lax.broadcasted_iota(jnp.int32, (SUBLANES, width), 0) == 0
    tail = halo_ref[:, cs]
    halo_ref[:, cs] = x[rows - (taps - 1) * SUBLANES:, :]
    wrapped = []
    for j in range(1, taps):
        cur = x[rows - j * SUBLANES:rows - (j - 1) * SUBLANES, :]
        prev = tail[(taps - 1 - j) * SUBLANES:(taps - j) * SUBLANES, :]
        wrapped.append(jnp.where(first_sublane, pltpu.roll(prev, 1, 0), pltpu.roll(cur, 1, 0)))
    out = cb_ref[:, cs] + cw_ref[taps - 1:taps, cs] * x
    for j in range(1, taps):
        shifted = jnp.concatenate(wrapped[j - 1::-1] + [x[:rows - j * SUBLANES, :]], axis=0)
        out = out + cw_ref[taps - 1 - j:taps - j, cs] * shifted
    return out


def _input_rows(x_refs, prefix_ref, t):
    first = jnp.where(t > 0, x_refs[0][0], prefix_ref[...])
    return jnp.concatenate([first] + [r[0] for r in x_refs[1:]], axis=0)


def _input_specs():
    n = ROW_TILE // PREFIX
    return [pl.BlockSpec((1, PREFIX, D_MODEL),
                         lambda b, t, k=k: (b, jnp.maximum(n * t - 1 + k, 0), 0))
            for k in range(n)] + [_const_spec((PREFIX, D_MODEL))]


def _mamba_front_kernel(xa_ref, xb_ref, xc_ref, prefix_ref,
                        wz_ref, wxbc_ref, wdt_ref, cw_ref, cb_ref, dtb_ref, alog_ref,
                        z_ref, xbc_ref, dt_ref, da_ref, halo_ref, buf_ref):
    t = pl.program_id(1)
    rows = ROW_TILE
    width = FRONT_CHUNK
    n_x = SSM_CONV_DIM // width
    n_z = SSM_D_INNER // width

    @pl.when(t == 0)
    def _():
        halo_ref[...] = jnp.zeros(halo_ref.shape, F32)

    u = _input_rows((xa_ref, xb_ref, xc_ref), prefix_ref, t).astype(BF16)

    def project(c):
        cs = slice(c * width, (c + 1) * width)
        slot = c % FRONT_BUFFERS
        buf_ref[slot, 0:HALO, :] = halo_ref[:, cs]
        pre = _dot(u, wxbc_ref[:, cs])
        buf_ref[slot, HALO:HALO + rows, :] = pre
        buf_ref[slot, HALO:HALO + PAD, :] = jnp.where(t > 0, pre[:PAD], 0.0)

    def gate(c):
        cs = slice(c * width, (c + 1) * width)
        z_ref[0, :, cs] = _dot(u, wz_ref[:, cs]).astype(BF16)

    for c in range(FRONT_BUFFERS - 1):
        project(c)
    for c in range(n_x):
        if c + FRONT_BUFFERS - 1 < n_x:
            project(c + FRONT_BUFFERS - 1)
        if c < n_z:
            gate(c)
        cs = slice(c * width, (c + 1) * width)
        slot = c % FRONT_BUFFERS
        acc = cb_ref[:, cs]
        for k in range(SSM_CONV):
            off = HALO - (SSM_CONV - 1) + k
            acc = acc + cw_ref[k:k + 1, cs] * buf_ref[slot, off:off + rows, :]
        halo_ref[:, cs] = buf_ref[slot, rows:rows + HALO, :]
        act = _silu(acc).astype(BF16)
        xbc_ref[0, :, cs] = act
        xbc_ref[0, 0:PAD, cs] = jnp.where(t > 0, act[:PAD], jnp.zeros((), BF16))

    x = _dot(u, wdt_ref[...]) + dtb_ref[...]
    dt = jnp.maximum(x, 0.0) + jnp.log1p(jnp.exp(-jnp.abs(x)))
    da = dt * (-jnp.exp(alog_ref[...]))
    dt_ref[0] = dt
    da_ref[0] = da
    dt_ref[0, 0:PAD, :] = jnp.where(t > 0, dt[:PAD], 0.0)
    da_ref[0, 0:PAD, :] = jnp.where(t > 0, da[:PAD], 0.0)


def _mamba_front(x, prefix, wz, wxbc, wdt, cw, cb, dtb, alog):
    bsz, seq, _ = x.shape
    lp = PREFIX + seq
    nt = lp // ROW_TILE
    row = lambda w: pl.BlockSpec((1, ROW_TILE, w), lambda b, t: (b, t, 0))
    return pl.pallas_call(
        _mamba_front_kernel,
        grid=(bsz, nt),
        in_specs=_input_specs() + [_const_spec(wz.shape), _const_spec(wxbc.shape),
                  _const_spec(wdt.shape), _const_spec(cw.shape), _const_spec(cb.shape),
                  _const_spec(dtb.shape), _const_spec(alog.shape)],
        out_specs=[row(SSM_D_INNER), row(SSM_CONV_DIM), row(HEAD_LANES), row(HEAD_LANES)],
        out_shape=[jax.ShapeDtypeStruct((bsz, lp, SSM_D_INNER), BF16),
                   jax.ShapeDtypeStruct((bsz, lp, SSM_CONV_DIM), BF16),
                   jax.ShapeDtypeStruct((bsz, lp, HEAD_LANES), F32),
                   jax.ShapeDtypeStruct((bsz, lp, HEAD_LANES), F32)],
        scratch_shapes=[pltpu.VMEM((HALO, SSM_CONV_DIM), F32),
                        pltpu.VMEM((FRONT_BUFFERS, ROW_TILE + HALO, FRONT_CHUNK), F32)],
        compiler_params=pltpu.CompilerParams(
            dimension_semantics=("arbitrary", "arbitrary"), vmem_limit_bytes=VMEM_LIMIT),
        name="mamba_front",
    )(x, x, x, prefix, wz, wxbc, wdt, cw, cb, dtb, alog)


def _mamba_scan_kernel(xa_ref, xb_ref, xc_ref, prefix_ref,
                       z_ref, xs_ref, b_ref, c_ref, dt_ref, da_ref, e_ref, dskip_ref,
                       ng_ref, wout_ref, lng_ref, lnb_ref, out_ref,
                       acs_ref, dte_ref, y_ref, state_ref):
    t = pl.program_id(1)
    rows = ROW_TILE
    n_chunks = rows // CHUNK

    @pl.when(t == 0)
    def _():
        state_ref[...] = jnp.zeros(state_ref.shape, F32)

    r_i = lax.broadcasted_iota(jnp.int32, (rows, rows), 0)
    c_i = lax.broadcasted_iota(jnp.int32, (rows, rows), 1)
    tri = jnp.where(((r_i >> 6) == (c_i >> 6)) & (c_i <= r_i), 1.0, 0.0).astype(BF16)
    acs = sum(_dot(tri, p) for p in _split_bf16(da_ref[0], 3))
    lane_copy = lax.broadcasted_iota(jnp.int32, (rows, HEAD_LANES), 1) // SSM_HEADS

    def packed_terms(v, parts):
        terms = _split_bf16(v, parts)
        out = jnp.zeros((rows, HEAD_LANES), BF16)
        for k, term in enumerate(terms):
            out = jnp.where(lane_copy == k, term, out)
        return out

    acs_ref[...] = _dot(packed_terms(acs, 3), e_ref[...]) * LOG2_E
    dte_ref[...] = _dot(packed_terms(dt_ref[0], 2), e_ref[...])

    s_i = lax.broadcasted_iota(jnp.int32, (CHUNK, 2 * CHUNK), 0)
    l_i = lax.broadcasted_iota(jnp.int32, (CHUNK, 2 * CHUNK), 1)
    diag = (l_i & (CHUNK - 1)) == s_i
    causal = (l_i & (CHUNK - 1)) <= s_i
    br_i = lax.broadcasted_iota(jnp.int32, (2 * CHUNK, 2 * CHUNK), 0)
    bc_i = lax.broadcasted_iota(jnp.int32, (2 * CHUNK, 2 * CHUNK), 1)
    blockdiag = (br_i >> 6) == (bc_i >> 6)

    def chunk_body(c, carry):
        r0 = pl.multiple_of(c * CHUNK, CHUNK)
        acs_c = acs_ref[pl.ds(r0, CHUNK), :]
        dte_c = dte_ref[pl.ds(r0, CHUNK), :]
        xs_c = xs_ref[0, pl.ds(r0, CHUNK), :]
        b_c = b_ref[0, pl.ds(r0, CHUNK), :]
        c_c = c_ref[0, pl.ds(r0, CHUNK), :]
        last = acs_c[CHUNK - 1:CHUNK, :]
        groups = [slice(g * SSM_STATE, (g + 1) * SSM_STATE) for g in range(SSM_GROUPS)]
        gcols = [slice(g * GROUP_COLS, (g + 1) * GROUP_COLS) for g in range(SSM_GROUPS)]
        pair = 2 * SSM_HEAD_DIM
        states = [state_ref[g] for g in range(SSM_GROUPS)]
        cb2 = [lax.dot_general(c_c[:, gl], jnp.concatenate([b_c[:, gl], b_c[:, gl]], axis=0),
                               (((1,), (1,)), ((), ())), preferred_element_type=F32)
               for gl in groups]
        y_off = [_dot(c_c[:, gl], st.astype(BF16)) for gl, st in zip(groups, states)]
        x_dt = xs_c.astype(F32) * dte_c
        xw = (x_dt * jnp.exp2(last - acs_c)).astype(BF16)
        x_dt = x_dt.astype(BF16)
        s_new = [lax.dot_general(b_c[:, gl], xw[:, gc], (((0,), (0,)), ((), ())),
                                 preferred_element_type=F32) for gl, gc in zip(groups, gcols)]
        grow = jnp.exp2(acs_c)
        chunk_decay = jnp.exp2(last)
        for g in range(SSM_GROUPS):
            for jj in range(GROUP_COLS // pair):
                pl_ = slice(g * GROUP_COLS + jj * pair, g * GROUP_COLS + (jj + 1) * pair)
                a_p = acs_c[:, pl_]
                a_row = jnp.sum(jnp.where(diag, a_p, 0.0), axis=0, keepdims=True)
                decay = jnp.exp2(jnp.where(causal, a_p - a_row, NEG_BIG))
                m2 = (cb2[g] * decay).astype(BF16)
                x_p = x_dt[:, pl_]
                x_bd = jnp.where(blockdiag, jnp.concatenate([x_p, x_p], axis=0),
                                 jnp.zeros((), BF16))
                y_ref[pl.ds(r0, CHUNK), pl_] = (
                    _dot(m2, x_bd) + y_off[g][:, jj * pair:(jj + 1) * pair] * grow[:, pl_])
            state_ref[g] = states[g] * chunk_decay[:, gcols[g]] + s_new[g]
        return carry

    lax.fori_loop(0, n_chunks, chunk_body, 0)

    y = y_ref[...] + xs_ref[0].astype(F32) * dskip_ref[...]
    yg = y * _silu(z_ref[0].astype(F32))
    parts = [_rms(yg[:, g * GROUP_COLS:(g + 1) * GROUP_COLS]) for g in range(SSM_GROUPS)]
    yn = jnp.concatenate(parts, axis=1) * ng_ref[...]
    mix = _dot(yn.astype(BF16), wout_ref[...])
    h = _input_rows((xa_ref, xb_ref, xc_ref), prefix_ref, t)
    out_ref[0] = _layer_norm(DN_ALPHA * h + mix, lng_ref[...], lnb_ref[...])


def _mamba_scan(x, prefix, z, xbc, dt, da, e, dskip, ng, wout, lng, lnb):
    bsz, seq, _ = x.shape
    lp = PREFIX + seq
    nt = lp // ROW_TILE
    row = lambda w, cb=0: pl.BlockSpec((1, ROW_TILE, w), lambda b, t: (b, t, cb))
    return pl.pallas_call(
        _mamba_scan_kernel,
        grid=(bsz, nt),
        in_specs=_input_specs() + [row(SSM_D_INNER), row(SSM_D_INNER),
                  row(SSM_BC, SSM_D_INNER // SSM_BC), row(SSM_BC, SSM_D_INNER // SSM_BC + 1),
                  row(HEAD_LANES), row(HEAD_LANES),
                  _const_spec(e.shape), _const_spec(dskip.shape), _const_spec(ng.shape),
                  _const_spec(wout.shape), _const_spec(lng.shape), _const_spec(lnb.shape)],
        out_specs=row(D_MODEL),
        out_shape=jax.ShapeDtypeStruct((bsz, lp, D_MODEL), F32),
        scratch_shapes=[pltpu.VMEM((ROW_TILE, SSM_D_INNER), F32),
                        pltpu.VMEM((ROW_TILE, SSM_D_INNER), F32),
                        pltpu.VMEM((ROW_TILE, SSM_D_INNER), F32),
                        pltpu.VMEM((SSM_GROUPS, SSM_STATE, GROUP_COLS), F32)],
        compiler_params=pltpu.CompilerParams(
            dimension_semantics=("arbitrary", "arbitrary"), vmem_limit_bytes=VMEM_LIMIT),
        name="mamba_scan",
    )(x, x, x, prefix, z, xbc, xbc, xbc, dt, da, e, dskip, ng, wout, lng, lnb)


def _ffn_body(h, t, win_ref, cw_ref, cb_ref, wout_ref, lng_ref, lnb_ref, out_ref,
              halo_ref, perm_ref):
    rows = ROW_TILE

    @pl.when(t == 0)
    def _():
        halo_ref[...] = jnp.zeros(halo_ref.shape, F32)

    h = _sublane_major(perm_ref, h)
    u = _matmul_rows(h, t)

    def cols(cb):
        return (slice(cb * FF_CHUNK, (cb + 1) * FF_CHUNK),
                slice(D_FF + cb * FF_CHUNK, D_FF + (cb + 1) * FF_CHUNK))

    def conv(x, cs):
        return _causal_conv(x, halo_ref, cw_ref, cb_ref, cs)

    def project(cb):
        return [_dot(u, win_ref[:, cs]) for cs in cols(cb)]

    pre = {cb: project(cb) for cb in range(FF_AHEAD)}
    acc = jnp.zeros((rows, D_MODEL), F32)
    for cb in range(N_FF_CHUNKS):
        if cb + FF_AHEAD < N_FF_CHUNKS:
            pre[cb + FF_AHEAD] = project(cb + FF_AHEAD)
        gate, up = pre.pop(cb)
        gate_cols, up_cols = cols(cb)
        act = (_silu(conv(gate, gate_cols)) * conv(up, up_cols)).astype(BF16)
        acc = acc + _dot(act, wout_ref[cb * FF_CHUNK:(cb + 1) * FF_CHUNK, :])
    out = _layer_norm(DN_ALPHA * h + acc, lng_ref[...], lnb_ref[...])
    out_ref[0] = _natural_order(perm_ref, out)


def _ffn_kernel(h_ref, win_ref, cw_ref, cb_ref, wout_ref, lng_ref, lnb_ref, out_ref,
                halo_ref, perm_ref):
    _ffn_body(h_ref[0], pl.program_id(1), win_ref, cw_ref, cb_ref, wout_ref, lng_ref, lnb_ref,
              out_ref, halo_ref, perm_ref)


def _attn_out_ffn_kernel(h_ref, o_ref, wo_ref, lng0_ref, lnb0_ref, win_ref, cw_ref, cb_ref,
                         wout_ref, lng_ref, lnb_ref, out_ref, halo_ref, perm_ref):
    mix = _dot(o_ref[0], wo_ref[...])
    h = _layer_norm(DN_ALPHA * h_ref[0] + mix, lng0_ref[...], lnb0_ref[...])
    _ffn_body(h, pl.program_id(1), win_ref, cw_ref, cb_ref, wout_ref, lng_ref, lnb_ref,
              out_ref, halo_ref, perm_ref)


def _ffn(h, win, cw, cb, wout, lng, lnb, attn=None):
    bsz, lp, _ = h.shape
    nt = lp // ROW_TILE
    row = pl.BlockSpec((1, ROW_TILE, D_MODEL), lambda b, t: (b, t, 0))
    consts = (win, cw, cb, wout, lng, lnb)
    if attn is None:
        kern, args, specs = _ffn_kernel, (h,) + consts, [row]
    else:
        o, wo, lng0, lnb0 = attn
        kern = _attn_out_ffn_kernel
        args = (h, o, wo, lng0, lnb0) + consts
        specs = [row, row, _const_spec(wo.shape), _const_spec(lng0.shape),
                 _const_spec(lnb0.shape)]
    return pl.pallas_call(
        kern,
        grid=(bsz, nt),
        in_specs=specs + [_const_spec(a.shape) for a in consts],
        out_specs=row,
        out_shape=jax.ShapeDtypeStruct((bsz, lp, D_MODEL), F32),
        scratch_shapes=[pltpu.VMEM((2 * SUBLANES, 2 * D_FF), F32),
                        pltpu.VMEM((D_MODEL // LANES, ROW_TILE, LANES), F32)],
        compiler_params=pltpu.CompilerParams(
            dimension_semantics=("arbitrary", "arbitrary"), vmem_limit_bytes=VMEM_LIMIT),
        name="conv_ffn" if attn is None else "attn_out_conv_ffn",
    )(*args)


def _mla_proj_kernel(h_ref, wkvd_ref, kvg_ref, wk_ref, wv_ref, wpe_ref, wper_ref, cosk_ref,
                     sink_ref, wqd_ref, qg_ref, wq_ref, wqr_ref, cosq_ref, sinq_ref,
                     q_ref, k_ref, v_ref):
    u = h_ref[0].astype(BF16)
    ckv = (_rms(_dot(u, wkvd_ref[...])) * kvg_ref[...]).astype(BF16)
    k_all = _dot(ckv, wk_ref[...])
    v_all = _dot(ckv, wv_ref[...])
    k_pe = _dot(u, wpe_ref[...]) * cosk_ref[...] + _dot(u, wper_ref[...]) * sink_ref[...]
    lane = lax.broadcasted_iota(jnp.int32, (1, HEAD_PAD), 1)
    ones_col = jnp.where(lane == V_HEAD, 1.0, 0.0)
    cq = (_rms(_dot(u, wqd_ref[...])) * qg_ref[...]).astype(BF16)
    q_all = _dot(cq, wq_ref[...])
    q_rot = _dot(cq, wqr_ref[...])
    cosq = cosq_ref[...]
    sinq = sinq_ref[...]
    for hd in range(MLA_HEADS):
        hl = slice(hd * HEAD_PAD, (hd + 1) * HEAD_PAD)
        k_ref[0, hd] = (k_all[:, hl] + k_pe).astype(BF16)
        v_ref[0, hd] = (v_all[:, hl] + ones_col).astype(BF16)
        q_ref[0, hd] = (q_all[:, hl] * cosq + q_rot[:, hl] * sinq).astype(BF16)


def _mla_proj(h, wkvd, kvg, wk, wv, wpe, wper, cosk, sink, wqd, qg, wq, wqr, cosq, sinq):
    bsz, lp, _ = h.shape
    nt = lp // ROW_TILE
    row = pl.BlockSpec((1, ROW_TILE, D_MODEL), lambda b, t: (b, t, 0))
    tab = pl.BlockSpec((ROW_TILE, HEAD_PAD), lambda b, t: (t, 0))
    head = pl.BlockSpec((1, MLA_HEADS, ROW_TILE, HEAD_PAD), lambda b, t: (b, 0, t, 0))
    shape = jax.ShapeDtypeStruct((bsz, MLA_HEADS, lp, HEAD_PAD), BF16)
    c = _const_spec
    return pl.pallas_call(
        _mla_proj_kernel,
        grid=(bsz, nt),
        in_specs=[row, c(wkvd.shape), c(kvg.shape), c(wk.shape), c(wv.shape), c(wpe.shape),
                  c(wper.shape), tab, tab, c(wqd.shape), c(qg.shape), c(wq.shape),
                  c(wqr.shape), tab, tab],
        out_specs=[head, head, head],
        out_shape=[shape, shape, shape],
        compiler_params=pltpu.CompilerParams(
            dimension_semantics=("arbitrary", "arbitrary"), vmem_limit_bytes=VMEM_LIMIT),
        name="mla_proj",
    )(h, wkvd, kvg, wk, wv, wpe, wper, cosk, sink, wqd, qg, wq, wqr, cosq, sinq)


def _attention_kernel(q_ref, k_ref, v_ref, o_ref):
    i = pl.program_id(2)
    rows = ROW_TILE
    r_i = lax.broadcasted_iota(jnp.int32, (rows, rows), 0)
    c_i = lax.broadcasted_iota(jnp.int32, (rows, rows), 1)
    qs = [q_ref[0, hh] for hh in range(ATT_HEADS)]

    def visible(j, kind):
        if kind is None:
            return None
        kpos = c_i + j * rows
        q_chunk = (r_i + i * rows) >> 6
        if kind == "diag":
            q_chunk = q_chunk - jnp.where(i > 0, 0, 1 << 20)
        return ((kpos >> 6) <= q_chunk) & (kpos >= PAD)

    def scores(j, hh, ok):
        k0 = pl.multiple_of(j * rows, rows)
        s = lax.dot_general(qs[hh], k_ref[0, hh, pl.ds(k0, rows), :],
                            (((1,), (1,)), ((), ())), preferred_element_type=F32)
        return s if ok is None else jnp.where(ok, s, NEG_BIG)

    def absorb(j, hh, s, state):
        k0 = pl.multiple_of(j * rows, rows)
        m, acc = state
        m_new = jnp.maximum(m, jnp.max(s, axis=-1, keepdims=True))
        p = jnp.exp2(s - m_new).astype(BF16)
        return m_new, jnp.exp2(m - m_new) * acc + _dot(p, v_ref[0, hh, pl.ds(k0, rows), :])

    def steps(tiles, carry):
        carry = list(carry)
        masks = [visible(j, kind) for j, kind in tiles]
        cur = [scores(tiles[0][0], hh, masks[0]) for hh in range(ATT_HEADS)]
        for n, (j, _) in enumerate(tiles):
            nxt = []
            for hh in range(ATT_HEADS):
                if n + 1 < len(tiles):
                    nxt.append(scores(tiles[n + 1][0], hh, masks[n + 1]))
                carry[hh] = absorb(j, hh, cur[hh], carry[hh])
            cur = nxt
        return tuple(carry)

    carry = tuple((jnp.full((rows, 1), NEG_BIG, F32), jnp.zeros((rows, HEAD_PAD), F32))
                  for _ in range(ATT_HEADS))
    n_mid = jnp.maximum(i - 1, 0)
    start = 1
    for width, trips in ((1, n_mid & 1), (2, (n_mid >> 1) & 1), (4, n_mid >> 2)):
        carry = lax.fori_loop(
            0, trips,
            lambda g, c, start=start, width=width: steps(
                [(start + width * g + w, None) for w in range(width)], c),
            carry)
        start = start + width * trips
    carry = steps([(0, "first"), (i, "diag")], carry)
    outs = [acc[:, :V_HEAD] / acc[:, V_HEAD:V_HEAD + 1] for _, acc in carry]
    o_ref[0] = jnp.concatenate(outs, axis=1).astype(BF16)


def _attention(q, k, v):
    bsz, nh, lp, _ = q.shape
    nt = lp // ROW_TILE
    hs = ATT_HEADS
    return pl.pallas_call(
        _attention_kernel,
        grid=(bsz, nh // hs, nt),
        in_specs=[pl.BlockSpec((1, hs, ROW_TILE, HEAD_PAD), lambda b, hp, i: (b, hp, i, 0)),
                  pl.BlockSpec((1, hs, lp, HEAD_PAD), lambda b, hp, i: (b, hp, 0, 0)),
                  pl.BlockSpec((1, hs, lp, HEAD_PAD), lambda b, hp, i: (b, hp, 0, 0))],
        out_specs=pl.BlockSpec((1, ROW_TILE, hs * V_HEAD), lambda b, hp, i: (b, i, hp)),
        out_shape=jax.ShapeDtypeStruct((bsz, lp, MLA_HEADS * V_HEAD), BF16),
        compiler_params=pltpu.CompilerParams(
            dimension_semantics=("arbitrary", "arbitrary", "arbitrary"),
            vmem_limit_bytes=VMEM_LIMIT),
        name="mla_attention",
    )(q, k, v)


def _row(v, width=None):
    v = v.astype(F32).reshape(1, -1)
    if width is not None and v.shape[1] < width:
        v = jnp.pad(v, ((0, 0), (0, width - v.shape[1])))
    return v


def _ffn_params(w_in, conv_w, conv_b, w_out):
    return w_in.astype(BF16), conv_w.astype(F32), _row(conv_b), w_out.astype(BF16)


def _rope_tables(lp):
    half = QK_ROPE // 2
    pos = jnp.maximum(jnp.arange(lp, dtype=F32) - PAD, 0.0)
    inv_freq = ROPE_THETA ** (-jnp.arange(half, dtype=F32) * 2.0 / QK_ROPE)
    ang = pos[:, None] * inv_freq[None, :]
    cos, sin = jnp.cos(ang), jnp.sin(ang)
    zeros = lambda n: jnp.zeros((lp, n), F32)
    tail = HEAD_PAD - QK_NOPE - QK_ROPE
    cosk = jnp.concatenate([zeros(QK_NOPE), cos, cos, zeros(tail)], axis=1)
    sink = jnp.concatenate([zeros(QK_NOPE), sin, sin, zeros(tail)], axis=1)
    c = (QK_NOPE + QK_ROPE) ** -0.5 * math.log2(math.e)
    cosq = c * jnp.concatenate([jnp.ones((lp, QK_NOPE), F32), cos, cos, zeros(tail)], axis=1)
    return cosk, sink, cosq, c * sink


def _rot_half_cols(w):
    half = QK_ROPE // 2
    return jnp.concatenate([-w[..., half:], w[..., :half]], axis=-1)


def _pad_last(a, before, total):
    pads = [(0, 0)] * (a.ndim - 1) + [(before, total - before - a.shape[-1])]
    return jnp.pad(a, pads)


def kernel(x, meta_tokens, m_w_in, m_conv_w, m_conv_b, m_dt_bias, m_a_log, m_d, m_norm_g, m_w_out, kv_w_down, kv_norm_g, kv_w_up, q_w_down, q_norm_g, q_w_up, o_w, f_w_in, f_conv_w, f_conv_b, f_w_out, ln_g, ln_b):
    bsz, seq, _ = x.shape
    lp = PAD + N_META + seq
    assert lp % ROW_TILE == 0 and PREFIX % CHUNK == 0 and ROW_TILE % PREFIX == 0
    assert seq % PREFIX == 0
    prefix = jnp.concatenate([jnp.zeros((PAD, D_MODEL), x.dtype), meta_tokens.astype(x.dtype)],
                             axis=0)

    w_in = m_w_in[0]
    wz = w_in[:, :SSM_D_INNER].astype(BF16)
    wxbc = w_in[:, SSM_D_INNER:SSM_D_INNER + SSM_CONV_DIM].astype(BF16)
    copies = HEAD_LANES // SSM_HEADS
    wdt = jnp.tile(w_in[:, SSM_D_INNER + SSM_CONV_DIM:], (1, copies)).astype(BF16)
    z, xbc, dt, da = _mamba_front(x, prefix, wz, wxbc, wdt, m_conv_w[0].astype(F32),
                                  _row(m_conv_b[0]), _row(jnp.tile(m_dt_bias[0], copies)),
                                  _row(jnp.tile(m_a_log[0], copies)))
    head_of_col = jnp.arange(SSM_D_INNER) // SSM_HEAD_DIM
    expand = (jnp.arange(HEAD_LANES)[:, None] % SSM_HEADS == head_of_col[None, :]).astype(BF16)
    h = _mamba_scan(x, prefix, z, xbc, dt, da, expand, _row(jnp.repeat(m_d[0], SSM_HEAD_DIM)),
                    _row(m_norm_g[0]), m_w_out[0].astype(BF16), _row(ln_g[0, 0]), _row(ln_b[0, 0]))
    h = _ffn(h, *_ffn_params(f_w_in[0], f_conv_w[0], f_conv_b[0], f_w_out[0]),
             _row(ln_g[0, 1]), _row(ln_b[0, 1]))

    cosk, sink, cosq, sinq = _rope_tables(lp)
    w_rope = kv_w_down[:, KV_LORA:]
    wpe = _pad_last(w_rope, QK_NOPE, HEAD_PAD).astype(BF16)
    wper = _pad_last(_rot_half_cols(w_rope), QK_NOPE, HEAD_PAD).astype(BF16)
    kv_up = kv_w_up.reshape(KV_LORA, MLA_HEADS, QK_NOPE + V_HEAD)
    wk = _pad_last(kv_up[..., :QK_NOPE], 0, HEAD_PAD).reshape(KV_LORA, -1).astype(BF16)
    wv = _pad_last(kv_up[..., QK_NOPE:], 0, HEAD_PAD).reshape(KV_LORA, -1).astype(BF16)
    q_up = q_w_up[0].reshape(Q_LORA, MLA_HEADS, QK_NOPE + QK_ROPE)
    wq = _pad_last(q_up, 0, HEAD_PAD).reshape(Q_LORA, -1).astype(BF16)
    wqr = _pad_last(_rot_half_cols(q_up[..., QK_NOPE:]), QK_NOPE, HEAD_PAD)
    wqr = wqr.reshape(Q_LORA, -1).astype(BF16)
    q, k, v = _mla_proj(h, kv_w_down[:, :KV_LORA].astype(BF16), _row(kv_norm_g), wk, wv, wpe,
                        wper, cosk, sink, q_w_down[0].astype(BF16), _row(q_norm_g[0]), wq, wqr,
                        cosq, sinq)
    o = _attention(q, k, v)
    h = _ffn(h, *_ffn_params(f_w_in[1], f_conv_w[1], f_conv_b[1], f_w_out[1]),
             _row(ln_g[1, 1]), _row(ln_b[1, 1]),
             attn=(o, o_w[0].astype(BF16), _row(ln_g[1, 0]), _row(ln_b[1, 0])))
    return h[:, PAD + N_META:]
```

```python
import functools
import math

import jax
import jax.numpy as jnp
from jax import lax
from jax.experimental import pallas as pl
from jax.experimental.pallas import tpu as pltpu

F32 = jnp.float32
BF16 = jnp.bfloat16

D_MODEL = 1024
N_META = 16
CHUNK = 64
DEPTH = 2
DN_ALPHA = (2.0 * DEPTH) ** 0.25
LN_EPS = 1e-5
RMS_EPS = 1e-6
LOG2_E = math.log2(math.e)

SSM_D_INNER = 2 * D_MODEL
SSM_HEAD_DIM = 64
SSM_HEADS = SSM_D_INNER // SSM_HEAD_DIM
SSM_GROUPS = 4
SSM_STATE = 128
SSM_CONV = 4
SSM_BC = SSM_GROUPS * SSM_STATE
SSM_CONV_DIM = SSM_D_INNER + 2 * SSM_BC
GROUP_COLS = SSM_D_INNER // SSM_GROUPS
HEAD_LANES = 128
FRONT_CHUNK = 512
FRONT_BUFFERS = 3

MLA_HEADS = 16
Q_LORA = 384
KV_LORA = 256
QK_NOPE = 64
QK_ROPE = 32
V_HEAD = 64
ROPE_THETA = 10000.0
HEAD_PAD = 128
ATT_HEADS = 8

D_FF = 2816
FFN_CONV = 3
FF_CHUNK = 256
N_FF_CHUNKS = D_FF // FF_CHUNK
FF_AHEAD = 2
SUBLANES = 8
LANES = 128

PAD = 112
PREFIX = PAD + N_META
ROW_TILE = 384
HALO = 8
NEG_BIG = -1e30
VMEM_LIMIT = 60 * 1024 * 1024


def _const_spec(shape):
    nd = len(shape)
    return pl.BlockSpec(shape, lambda *_: (0,) * nd, pipeline_mode=pl.Buffered(1))


def _layer_norm(v, g, b):
    mu = jnp.mean(v, axis=-1, keepdims=True)
    d = v - mu
    var = jnp.mean(d * d, axis=-1, keepdims=True)
    return d * lax.rsqrt(var + LN_EPS) * g + b


def _rms(v):
    return v * lax.rsqrt(jnp.mean(v * v, axis=-1, keepdims=True) + RMS_EPS)


def _silu(v):
    return v / (1.0 + jnp.exp(-v))


def _split_bf16(v, parts):
    out = []
    rem = v
    for _ in range(parts):
        p = rem.astype(BF16)
        out.append(p)
        rem = rem - p.astype(F32)
    return out


def _dot(a, b):
    return jnp.dot(a, b, preferred_element_type=F32)


def _dot_split(lhs_parts, rhs):
    acc = _dot(lhs_parts[0], rhs)
    for p in lhs_parts[1:]:
        acc = acc + _dot(p, rhs)
    return acc


def _sublane_major(perm_ref, v):
    rows, width = v.shape
    stride = rows // SUBLANES
    n = width // LANES
    for s in range(n):
        perm_ref[s] = v[:, s * LANES:(s + 1) * LANES]
    return jnp.concatenate(
        [jnp.concatenate([perm_ref[s, pl.ds(a, SUBLANES, stride=stride), :]
                          for s in range(n)], axis=1) for a in range(stride)], axis=0)


def _natural_order(perm_ref, v):
    rows, width = v.shape
    stride = rows // SUBLANES
    n = width // LANES
    for a in range(stride):
        for s in range(n):
            perm_ref[s, pl.ds(a, SUBLANES, stride=stride), :] = (
                v[a * SUBLANES:(a + 1) * SUBLANES, s * LANES:(s + 1) * LANES])
    return jnp.concatenate([perm_ref[s] for s in range(n)], axis=1)


def _matmul_rows(h, t):
    rows, width = h.shape
    p_i = lax.broadcasted_iota(jnp.int32, (rows, LANES), 0)
    time = (p_i & (SUBLANES - 1)) * (rows // SUBLANES) + (p_i >> 3)
    keep = time >= jnp.where(t > 0, 0, PAD)
    return jnp.concatenate([jnp.where(keep, h[:, s * LANES:(s + 1) * LANES], 0.0)
                            for s in range(width // LANES)], axis=1).astype(BF16)


def _causal_conv(x, halo_ref, cw_ref, cb_ref, cs):
    rows, width = x.shape
    taps = cw_ref.shape[0]
    first_sublane = lax.broadcasted_iota(jnp.int32, (SUBLANES, width), 0) == 0
    tail = halo_ref[:, cs]
    halo_ref[:, cs] = x[rows - (taps - 1) * SUBLANES:, :]
    wrapped = []
    for j in range(1, taps):
        cur = x[rows - j * SUBLANES:rows - (j - 1) * SUBLANES, :]
        prev = tail[(taps - 1 - j) * SUBLANES:(taps - j) * SUBLANES, :]
        wrapped.append(jnp.where(first_sublane, pltpu.roll(prev, 1, 0), pltpu.roll(cur, 1, 0)))
    out = cb_ref[:, cs] + cw_ref[taps - 1:taps, cs] * x
    for j in range(1, taps):
        shifted = jnp.concatenate(wrapped[j - 1::-1] + [x[:rows - j * SUBLANES, :]], axis=0)
        out = out + cw_ref[taps - 1 - j:taps - j, cs] * shifted
    return out


def _input_rows(x_refs, prefix_ref, t):
    first = jnp.where(t > 0, x_refs[0][0], prefix_ref[...])
    return jnp.concatenate([first] + [r[0] for r in x_refs[1:]], axis=0)


def _input_specs():
    n = ROW_TILE // PREFIX
    return [pl.BlockSpec((1, PREFIX, D_MODEL),
                         lambda b, t, k=k: (b, jnp.maximum(n * t - 1 + k, 0), 0))
            for k in range(n)] + [_const_spec((PREFIX, D_MODEL))]


def _mamba_front_kernel(xa_ref, xb_ref, xc_ref, prefix_ref,
                        wz_ref, wxbc_ref, wdt_ref, cw_ref, cb_ref, dtb_ref, alog_ref,
                        z_ref, xbc_ref, dt_ref, da_ref, halo_ref, buf_ref):
    t = pl.program_id(1)
    rows = ROW_TILE
    width = FRONT_CHUNK
    n_x = SSM_CONV_DIM // width
    n_z = SSM_D_INNER // width

    @pl.when(t == 0)
    def _():
        halo_ref[...] = jnp.zeros(halo_ref.shape, F32)

    u = _input_rows((xa_ref, xb_ref, xc_ref), prefix_ref, t).astype(BF16)

    def project(c):
        cs = slice(c * width, (c + 1) * width)
        slot = c % FRONT_BUFFERS
        buf_ref[slot, 0:HALO, :] = halo_ref[:, cs]
        pre = _dot(u, wxbc_ref[:, cs])
        buf_ref[slot, HALO:HALO + rows, :] = pre
        buf_ref[slot, HALO:HALO + PAD, :] = jnp.where(t > 0, pre[:PAD], 0.0)

    def gate(c):
        cs = slice(c * width, (c + 1) * width)
        z_ref[0, :, cs] = _dot(u, wz_ref[:, cs]).astype(BF16)

    for c in range(FRONT_BUFFERS - 1):
        project(c)
    for c in range(n_x):
        if c + FRONT_BUFFERS - 1 < n_x:
            project(c + FRONT_BUFFERS - 1)
        if c < n_z:
            gate(c)
        cs = slice(c * width, (c + 1) * width)
        slot = c % FRONT_BUFFERS
        acc = cb_ref[:, cs]
        for k in range(SSM_CONV):
            off = HALO - (SSM_CONV - 1) + k
            acc = acc + cw_ref[k:k + 1, cs] * buf_ref[slot, off:off + rows, :]
        halo_ref[:, cs] = buf_ref[slot, rows:rows + HALO, :]
        act = _silu(acc).astype(BF16)
        xbc_ref[0, :, cs] = act
        xbc_ref[0, 0:PAD, cs] = jnp.where(t > 0, act[:PAD], jnp.zeros((), BF16))

    x = _dot(u, wdt_ref[...]) + dtb_ref[...]
    dt = jnp.maximum(x, 0.0) + jnp.log1p(jnp.exp(-jnp.abs(x)))
    da = dt * (-jnp.exp(alog_ref[...]) * LOG2_E)
    dt_ref[0] = dt
    da_ref[0] = da
    dt_ref[0, 0:PAD, :] = jnp.where(t > 0, dt[:PAD], 0.0)
    da_ref[0, 0:PAD, :] = jnp.where(t > 0, da[:PAD], 0.0)


def _mamba_front(x, prefix, wz, wxbc, wdt, cw, cb, dtb, alog):
    bsz, seq, _ = x.shape
    lp = PREFIX + seq
    nt = lp // ROW_TILE
    row = lambda w: pl.BlockSpec((1, ROW_TILE, w), lambda b, t: (b, t, 0))
    return pl.pallas_call(
        _mamba_front_kernel,
        grid=(bsz, nt),
        in_specs=_input_specs() + [_const_spec(wz.shape), _const_spec(wxbc.shape),
                  _const_spec(wdt.shape), _const_spec(cw.shape), _const_spec(cb.shape),
                  _const_spec(dtb.shape), _const_spec(alog.shape)],
        out_specs=[row(SSM_D_INNER), row(SSM_CONV_DIM), row(HEAD_LANES), row(HEAD_LANES)],
        out_shape=[jax.ShapeDtypeStruct((bsz, lp, SSM_D_INNER), BF16),
                   jax.ShapeDtypeStruct((bsz, lp, SSM_CONV_DIM), BF16),
                   jax.ShapeDtypeStruct((bsz, lp, HEAD_LANES), F32),
                   jax.ShapeDtypeStruct((bsz, lp, HEAD_LANES), F32)],
        scratch_shapes=[pltpu.VMEM((HALO, SSM_CONV_DIM), F32),
                        pltpu.VMEM((FRONT_BUFFERS, ROW_TILE + HALO, FRONT_CHUNK), F32)],
        compiler_params=pltpu.CompilerParams(
            dimension_semantics=("arbitrary", "arbitrary"), vmem_limit_bytes=VMEM_LIMIT),
        name="mamba_front",
    )(x, x, x, prefix, wz, wxbc, wdt, cw, cb, dtb, alog)


def _mamba_scan_kernel(xa_ref, xb_ref, xc_ref, prefix_ref,
                       z_ref, xs_ref, b_ref, c_ref, dt_ref, da_ref, e_ref, dskip_ref,
                       ng_ref, wout_ref, lng_ref, lnb_ref, out_ref,
                       acs_ref, dte_ref, y_ref, state_ref):
    t = pl.program_id(1)
    rows = ROW_TILE
    n_chunks = rows // CHUNK

    @pl.when(t == 0)
    def _():
        state_ref[...] = jnp.zeros(state_ref.shape, F32)

    r_i = lax.broadcasted_iota(jnp.int32, (rows, rows), 0)
    c_i = lax.broadcasted_iota(jnp.int32, (rows, rows), 1)
    tri = jnp.where(((r_i >> 6) == (c_i >> 6)) & (c_i <= r_i), 1.0, 0.0).astype(BF16)
    acs = sum(_dot(tri, p) for p in _split_bf16(da_ref[0], 3))
    lane_copy = lax.broadcasted_iota(jnp.int32, (rows, HEAD_LANES), 1) // SSM_HEADS

    def packed_terms(v, parts):
        terms = _split_bf16(v, parts)
        out = jnp.zeros((rows, HEAD_LANES), BF16)
        for k, term in enumerate(terms):
            out = jnp.where(lane_copy == k, term, out)
        return out

    acs_ref[...] = _dot(packed_terms(acs, 3), e_ref[...])
    dte_ref[...] = _dot(packed_terms(dt_ref[0], 2), e_ref[...])

    s_i = lax.broadcasted_iota(jnp.int32, (CHUNK, 2 * CHUNK), 0)
    l_i = lax.broadcasted_iota(jnp.int32, (CHUNK, 2 * CHUNK), 1)
    diag = (l_i & (CHUNK - 1)) == s_i
    causal = (l_i & (CHUNK - 1)) <= s_i
    br_i = lax.broadcasted_iota(jnp.int32, (2 * CHUNK, 2 * CHUNK), 0)
    bc_i = lax.broadcasted_iota(jnp.int32, (2 * CHUNK, 2 * CHUNK), 1)
    blockdiag = (br_i >> 6) == (bc_i >> 6)

    def chunk_body(c, carry):
        r0 = pl.multiple_of(c * CHUNK, CHUNK)
        acs_c = acs_ref[pl.ds(r0, CHUNK), :]
        dte_c = dte_ref[pl.ds(r0, CHUNK), :]
        xs_c = xs_ref[0, pl.ds(r0, CHUNK), :]
        b_c = b_ref[0, pl.ds(r0, CHUNK), :]
        c_c = c_ref[0, pl.ds(r0, CHUNK), :]
        last = acs_c[CHUNK - 1:CHUNK, :]
        groups = [slice(g * SSM_STATE, (g + 1) * SSM_STATE) for g in range(SSM_GROUPS)]
        gcols = [slice(g * GROUP_COLS, (g + 1) * GROUP_COLS) for g in range(SSM_GROUPS)]
        pair = 2 * SSM_HEAD_DIM
        states = [state_ref[g] for g in range(SSM_GROUPS)]
        cb2 = [lax.dot_general(c_c[:, gl], jnp.concatenate([b_c[:, gl], b_c[:, gl]], axis=0),
                               (((1,), (1,)), ((), ())), preferred_element_type=F32)
               for gl in groups]
        y_off = [_dot(c_c[:, gl], st.astype(BF16)) for gl, st in zip(groups, states)]
        x_dt = xs_c.astype(F32) * dte_c
        xw = (x_dt * jnp.exp2(last - acs_c)).astype(BF16)
        x_dt = x_dt.astype(BF16)
        s_new = [lax.dot_general(b_c[:, gl], xw[:, gc], (((0,), (0,)), ((), ())),
                                 preferred_element_type=F32) for gl, gc in zip(groups, gcols)]
        grow = jnp.exp2(acs_c)
        chunk_decay = jnp.exp2(last)
        for g in range(SSM_GROUPS):
            for jj in range(GROUP_COLS // pair):
                pl_ = slice(g * GROUP_COLS + jj * pair, g * GROUP_COLS + (jj + 1) * pair)
                a_p = acs_c[:, pl_]
                a_row = jnp.sum(jnp.where(diag, a_p, 0.0), axis=0, keepdims=True)
                decay = jnp.exp2(jnp.where(causal, a_p - a_row, NEG_BIG))
                m2 = (cb2[g] * decay).astype(BF16)
                x_p = x_dt[:, pl_]
                x_bd = jnp.where(blockdiag, jnp.concatenate([x_p, x_p], axis=0),
                                 jnp.zeros((), BF16))
                y_ref[pl.ds(r0, CHUNK), pl_] = (
                    _dot(m2, x_bd) + y_off[g][:, jj * pair:(jj + 1) * pair] * grow[:, pl_])
            state_ref[g] = states[g] * chunk_decay[:, gcols[g]] + s_new[g]
        return carry

    lax.fori_loop(0, n_chunks, chunk_body, 0)

    mix = jnp.zeros((rows, D_MODEL), F32)
    for g in range(SSM_GROUPS):
        gc = slice(g * GROUP_COLS, (g + 1) * GROUP_COLS)
        y = y_ref[:, gc] + xs_ref[0, :, gc].astype(F32) * dskip_ref[:, gc]
        yn = _rms(y * _silu(z_ref[0, :, gc].astype(F32))) * ng_ref[:, gc]
        mix = mix + _dot(yn.astype(BF16), wout_ref[gc, :])
    h = _input_rows((xa_ref, xb_ref, xc_ref), prefix_ref, t)
    out_ref[0] = _layer_norm(DN_ALPHA * h + mix, lng_ref[...], lnb_ref[...])


def _mamba_scan(x, prefix, z, xbc, dt, da, e, dskip, ng, wout, lng, lnb):
    bsz, seq, _ = x.shape
    lp = PREFIX + seq
    nt = lp // ROW_TILE
    row = lambda w, cb=0: pl.BlockSpec((1, ROW_TILE, w), lambda b, t: (b, t, cb))
    return pl.pallas_call(
        _mamba_scan_kernel,
        grid=(bsz, nt),
        in_specs=_input_specs() + [row(SSM_D_INNER), row(SSM_D_INNER),
                  row(SSM_BC, SSM_D_INNER // SSM_BC), row(SSM_BC, SSM_D_INNER // SSM_BC + 1),
                  row(HEAD_LANES), row(HEAD_LANES),
                  _const_spec(e.shape), _const_spec(dskip.shape), _const_spec(ng.shape),
                  _const_spec(wout.shape), _const_spec(lng.shape), _const_spec(lnb.shape)],
        out_specs=row(D_MODEL),
        out_shape=jax.ShapeDtypeStruct((bsz, lp, D_MODEL), F32),
        scratch_shapes=[pltpu.VMEM((ROW_TILE, SSM_D_INNER), F32),
                        pltpu.VMEM((ROW_TILE, SSM_D_INNER), F32),
                        pltpu.VMEM((ROW_TILE, SSM_D_INNER), F32),
                        pltpu.VMEM((SSM_GROUPS, SSM_STATE, GROUP_COLS), F32)],
        compiler_params=pltpu.CompilerParams(
            dimension_semantics=("arbitrary", "arbitrary"), vmem_limit_bytes=VMEM_LIMIT),
        name="mamba_scan",
    )(x, x, x, prefix, z, xbc, xbc, xbc, dt, da, e, dskip, ng, wout, lng, lnb)


def _ffn_body(h, t, win_ref, cw_ref, cb_ref, wout_ref, lng_ref, lnb_ref, out_ref,
              halo_ref, perm_ref):
    rows = ROW_TILE

    @pl.when(t == 0)
    def _():
        halo_ref[...] = jnp.zeros(halo_ref.shape, F32)

    h = _sublane_major(perm_ref, h)
    u = _matmul_rows(h, t)

    def cols(cb):
        return (slice(cb * FF_CHUNK, (cb + 1) * FF_CHUNK),
                slice(D_FF + cb * FF_CHUNK, D_FF + (cb + 1) * FF_CHUNK))

    def conv(x, cs):
        return _causal_conv(x, halo_ref, cw_ref, cb_ref, cs)

    def project(cb):
        return [_dot(u, win_ref[:, cs]) for cs in cols(cb)]

    pre = {cb: project(cb) for cb in range(FF_AHEAD)}
    acc = jnp.zeros((rows, D_MODEL), F32)
    for cb in range(N_FF_CHUNKS):
        if cb + FF_AHEAD < N_FF_CHUNKS:
            pre[cb + FF_AHEAD] = project(cb + FF_AHEAD)
        gate, up = pre.pop(cb)
        gate_cols, up_cols = cols(cb)
        act = (_silu(conv(gate, gate_cols)) * conv(up, up_cols)).astype(BF16)
        acc = acc + _dot(act, wout_ref[cb * FF_CHUNK:(cb + 1) * FF_CHUNK, :])
    out = _layer_norm(DN_ALPHA * h + acc, lng_ref[...], lnb_ref[...])
    out_ref[0] = _natural_order(perm_ref, out)


def _ffn_kernel(h_ref, win_ref, cw_ref, cb_ref, wout_ref, lng_ref, lnb_ref, out_ref,
                halo_ref, perm_ref):
    _ffn_body(h_ref[0], pl.program_id(1), win_ref, cw_ref, cb_ref, wout_ref, lng_ref, lnb_ref,
              out_ref, halo_ref, perm_ref)


def _attn_out_ffn_kernel(h_ref, o_ref, wo_ref, lng0_ref, lnb0_ref, win_ref, cw_ref, cb_ref,
                         wout_ref, lng_ref, lnb_ref, out_ref, halo_ref, perm_ref):
    mix = _dot(o_ref[0], wo_ref[...])
    h = _layer_norm(DN_ALPHA * h_ref[0] + mix, lng0_ref[...], lnb0_ref[...])
    _ffn_body(h, pl.program_id(1), win_ref, cw_ref, cb_ref, wout_ref, lng_ref, lnb_ref,
              out_ref, halo_ref, perm_ref)


def _ffn(h, win, cw, cb, wout, lng, lnb, attn=None):
    bsz, lp, _ = h.shape
    nt = lp // ROW_TILE
    row = pl.BlockSpec((1, ROW_TILE, D_MODEL), lambda b, t: (b, t, 0))
    consts = (win, cw, cb, wout, lng, lnb)
    if attn is None:
        kern, args, specs = _ffn_kernel, (h,) + consts, [row]
    else:
        o, wo, lng0, lnb0 = attn
        kern = _attn_out_ffn_kernel
        args = (h, o, wo, lng0, lnb0) + consts
        specs = [row, row, _const_spec(wo.shape), _const_spec(lng0.shape),
                 _const_spec(lnb0.shape)]
    return pl.pallas_call(
        kern,
        grid=(bsz, nt),
        in_specs=specs + [_const_spec(a.shape) for a in consts],
        out_specs=row,
        out_shape=jax.ShapeDtypeStruct((bsz, lp, D_MODEL), F32),
        scratch_shapes=[pltpu.VMEM((2 * SUBLANES, 2 * D_FF), F32),
                        pltpu.VMEM((D_MODEL // LANES, ROW_TILE, LANES), F32)],
        compiler_params=pltpu.CompilerParams(
            dimension_semantics=("arbitrary", "arbitrary"), vmem_limit_bytes=VMEM_LIMIT),
        name="conv_ffn" if attn is None else "attn_out_conv_ffn",
    )(*args)


def _mla_proj_kernel(h_ref, wkvd_ref, kvg_ref, wk_ref, wv_ref, wpe_ref, cosk_ref,
                     sink_ref, wqd_ref, qg_ref, wq_ref, cosq_ref, sinq_ref,
                     q_ref, k_ref, v_ref):
    half = QK_ROPE // 2
    lane = lax.broadcasted_iota(jnp.int32, (1, HEAD_PAD), 1)
    first_half = lane < QK_NOPE + half

    def rope(x, cos, sin_signed):
        swapped = jnp.where(first_half, pltpu.roll(x, HEAD_PAD - half, 1),
                            pltpu.roll(x, half, 1))
        return x * cos + swapped * sin_signed

    u = h_ref[0].astype(BF16)
    cq = (_rms(_dot(u, wqd_ref[...])) * qg_ref[...]).astype(BF16)
    q_all = _dot(cq, wq_ref[...])
    k_pe = rope(_dot(u, wpe_ref[...]), cosk_ref[...], sink_ref[...])
    ckv = (_rms(_dot(u, wkvd_ref[...])) * kvg_ref[...]).astype(BF16)
    k_all = _dot(ckv, wk_ref[...])
    v_all = _dot(ckv, wv_ref[...])
    ones_col = jnp.where(lane == V_HEAD, 1.0, 0.0)
    cosq = cosq_ref[...]
    sinq = sinq_ref[...]
    for hd in range(MLA_HEADS):
        hl = slice(hd * HEAD_PAD, (hd + 1) * HEAD_PAD)
        q_ref[0, hd] = rope(q_all[:, hl], cosq, sinq).astype(BF16)
    for hd in range(MLA_HEADS):
        hl = slice(hd * HEAD_PAD, (hd + 1) * HEAD_PAD)
        k_ref[0, hd] = (k_all[:, hl] + k_pe).astype(BF16)
        v_ref[0, hd] = (v_all[:, hl] + ones_col).astype(BF16)


def _mla_proj(h, wkvd, kvg, wk, wv, wpe, cosk, sink, wqd, qg, wq, cosq, sinq):
    bsz, lp, _ = h.shape
    nt = lp // ROW_TILE
    row = pl.BlockSpec((1, ROW_TILE, D_MODEL), lambda b, t: (b, t, 0))
    tab = pl.BlockSpec((ROW_TILE, HEAD_PAD), lambda b, t: (t, 0))
    head = pl.BlockSpec((1, MLA_HEADS, ROW_TILE, HEAD_PAD), lambda b, t: (b, 0, t, 0))
    shape = jax.ShapeDtypeStruct((bsz, MLA_HEADS, lp, HEAD_PAD), BF16)
    c = _const_spec
    return pl.pallas_call(
        _mla_proj_kernel,
        grid=(bsz, nt),
        in_specs=[row, c(wkvd.shape), c(kvg.shape), c(wk.shape), c(wv.shape), c(wpe.shape),
                  tab, tab, c(wqd.shape), c(qg.shape), c(wq.shape), tab, tab],
        out_specs=[head, head, head],
        out_shape=[shape, shape, shape],
        compiler_params=pltpu.CompilerParams(
            dimension_semantics=("arbitrary", "arbitrary"), vmem_limit_bytes=VMEM_LIMIT),
        name="mla_proj",
    )(h, wkvd, kvg, wk, wv, wpe, cosk, sink, wqd, qg, wq, cosq, sinq)


def _attention_kernel(q_ref, k_ref, v_ref, o_ref):
    i = pl.program_id(2)
    rows = ROW_TILE
    r_i = lax.broadcasted_iota(jnp.int32, (rows, rows), 0)
    c_i = lax.broadcasted_iota(jnp.int32, (rows, rows), 1)
    qs = [q_ref[0, hh] for hh in range(ATT_HEADS)]

    def visible(j, kind):
        if kind is None:
            return None
        kpos = c_i + j * rows
        q_chunk = (r_i + i * rows) >> 6
        if kind == "diag":
            q_chunk = q_chunk - jnp.where(i > 0, 0, 1 << 20)
        return ((kpos >> 6) <= q_chunk) & (kpos >= PAD)

    def scores(j, hh, ok):
        k0 = pl.multiple_of(j * rows, rows)
        s = lax.dot_general(qs[hh], k_ref[0, hh, pl.ds(k0, rows), :],
                            (((1,), (1,)), ((), ())), preferred_element_type=F32)
        return s if ok is None else jnp.where(ok, s, NEG_BIG)

    def absorb(j, hh, s, state):
        k0 = pl.multiple_of(j * rows, rows)
        m, acc = state
        m_new = jnp.maximum(m, jnp.max(s, axis=-1, keepdims=True))
        p = jnp.exp2(s - m_new).astype(BF16)
        return m_new, jnp.exp2(m - m_new) * acc + _dot(p, v_ref[0, hh, pl.ds(k0, rows), :])

    def steps(tiles, carry):
        carry = list(carry)
        masks = [visible(j, kind) for j, kind in tiles]
        cur = [scores(tiles[0][0], hh, masks[0]) for hh in range(ATT_HEADS)]
        for n, (j, _) in enumerate(tiles):
            nxt = []
            for hh in range(ATT_HEADS):
                if n + 1 < len(tiles):
                    nxt.append(scores(tiles[n + 1][0], hh, masks[n + 1]))
                carry[hh] = absorb(j, hh, cur[hh], carry[hh])
            cur = nxt
        return tuple(carry)

    carry = tuple((jnp.full((rows, 1), NEG_BIG, F32), jnp.zeros((rows, HEAD_PAD), F32))
                  for _ in range(ATT_HEADS))
    n_mid = jnp.maximum(i - 1, 0)
    start = 1
    for width, trips in ((1, n_mid & 1), (2, (n_mid >> 1) & 1), (4, n_mid >> 2)):
        carry = lax.fori_loop(
            0, trips,
            lambda g, c, start=start, width=width: steps(
                [(start + width * g + w, None) for w in range(width)], c),
            carry)
        start = start + width * trips
    carry = steps([(0, "first"), (i, "diag")], carry)
    outs = [acc[:, :V_HEAD] / acc[:, V_HEAD:V_HEAD + 1] for _, acc in carry]
    o_ref[0] = jnp.concatenate(outs, axis=1).astype(BF16)


def _attention(q, k, v):
    bsz, nh, lp, _ = q.shape
    nt = lp // ROW_TILE
    hs = ATT_HEADS
    return pl.pallas_call(
        _attention_kernel,
        grid=(bsz, nh // hs, nt),
        in_specs=[pl.BlockSpec((1, hs, ROW_TILE, HEAD_PAD), lambda b, hp, i: (b, hp, i, 0)),
                  pl.BlockSpec((1, hs, lp, HEAD_PAD), lambda b, hp, i: (b, hp, 0, 0)),
                  pl.BlockSpec((1, hs, lp, HEAD_PAD), lambda b, hp, i: (b, hp, 0, 0))],
        out_specs=pl.BlockSpec((1, ROW_TILE, hs * V_HEAD), lambda b, hp, i: (b, i, hp)),
        out_shape=jax.ShapeDtypeStruct((bsz, lp, MLA_HEADS * V_HEAD), BF16),
        compiler_params=pltpu.CompilerParams(
            dimension_semantics=("arbitrary", "arbitrary", "arbitrary"),
            vmem_limit_bytes=VMEM_LIMIT),
        name="mla_attention",
    )(q, k, v)


def _row(v, width=None):
    v = v.astype(F32).reshape(1, -1)
    if width is not None and v.shape[1] < width:
        v = jnp.pad(v, ((0, 0), (0, width - v.shape[1])))
    return v


def _ffn_params(w_in, conv_w, conv_b, w_out):
    return w_in.astype(BF16), conv_w.astype(F32), _row(conv_b), w_out.astype(BF16)


def _rope_tables(lp):
    half = QK_ROPE // 2
    pos = jnp.maximum(jnp.arange(lp, dtype=F32) - PAD, 0.0)
    inv_freq = ROPE_THETA ** (-jnp.arange(half, dtype=F32) * 2.0 / QK_ROPE)
    ang = pos[:, None] * inv_freq[None, :]
    cos, sin = jnp.cos(ang), jnp.sin(ang)
    zeros = lambda n: jnp.zeros((lp, n), F32)
    tail = HEAD_PAD - QK_NOPE - QK_ROPE
    cosk = jnp.concatenate([zeros(QK_NOPE), cos, cos, zeros(tail)], axis=1)
    sink = jnp.concatenate([zeros(QK_NOPE), -sin, sin, zeros(tail)], axis=1)
    c = (QK_NOPE + QK_ROPE) ** -0.5 * LOG2_E
    cosq = c * jnp.concatenate([jnp.ones((lp, QK_NOPE), F32), cos, cos, zeros(tail)], axis=1)
    return cosk, sink, cosq, c * sink


def _pad_last(a, before, total):
    pads = [(0, 0)] * (a.ndim - 1) + [(before, total - before - a.shape[-1])]
    return jnp.pad(a, pads)


def kernel(x, meta_tokens, m_w_in, m_conv_w, m_conv_b, m_dt_bias, m_a_log, m_d, m_norm_g, m_w_out, kv_w_down, kv_norm_g, kv_w_up, q_w_down, q_norm_g, q_w_up, o_w, f_w_in, f_conv_w, f_conv_b, f_w_out, ln_g, ln_b):
    bsz, seq, _ = x.shape
    lp = PAD + N_META + seq
    assert lp % ROW_TILE == 0 and PREFIX % CHUNK == 0 and ROW_TILE % PREFIX == 0
    assert seq % PREFIX == 0
    prefix = jnp.concatenate([jnp.zeros((PAD, D_MODEL), x.dtype), meta_tokens.astype(x.dtype)],
                             axis=0)

    w_in = m_w_in[0]
    wz = w_in[:, :SSM_D_INNER].astype(BF16)
    wxbc = w_in[:, SSM_D_INNER:SSM_D_INNER + SSM_CONV_DIM].astype(BF16)
    copies = HEAD_LANES // SSM_HEADS
    wdt = jnp.tile(w_in[:, SSM_D_INNER + SSM_CONV_DIM:], (1, copies)).astype(BF16)
    z, xbc, dt, da = _mamba_front(x, prefix, wz, wxbc, wdt, m_conv_w[0].astype(F32),
                                  _row(m_conv_b[0]), _row(jnp.tile(m_dt_bias[0], copies)),
                                  _row(jnp.tile(m_a_log[0], copies)))
    head_of_col = jnp.arange(SSM_D_INNER) // SSM_HEAD_DIM
    expand = (jnp.arange(HEAD_LANES)[:, None] % SSM_HEADS == head_of_col[None, :]).astype(BF16)
    h = _mamba_scan(x, prefix, z, xbc, dt, da, expand, _row(jnp.repeat(m_d[0], SSM_HEAD_DIM)),
                    _row(m_norm_g[0]), m_w_out[0].astype(BF16), _row(ln_g[0, 0]), _row(ln_b[0, 0]))
    h = _ffn(h, *_ffn_params(f_w_in[0], f_conv_w[0], f_conv_b[0], f_w_out[0]),
             _row(ln_g[0, 1]), _row(ln_b[0, 1]))

    cosk, sink, cosq, sinq = _rope_tables(lp)
    w_rope = kv_w_down[:, KV_LORA:]
    wpe = _pad_last(w_rope, QK_NOPE, HEAD_PAD).astype(BF16)
    kv_up = kv_w_up.reshape(KV_LORA, MLA_HEADS, QK_NOPE + V_HEAD)
    wk = _pad_last(kv_up[..., :QK_NOPE], 0, HEAD_PAD).reshape(KV_LORA, -1).astype(BF16)
    wv = _pad_last(kv_up[..., QK_NOPE:], 0, HEAD_PAD).reshape(KV_LORA, -1).astype(BF16)
    q_up = q_w_up[0].reshape(Q_LORA, MLA_HEADS, QK_NOPE + QK_ROPE)
    wq = _pad_last(q_up, 0, HEAD_PAD).reshape(Q_LORA, -1).astype(BF16)
    q, k, v = _mla_proj(h, kv_w_down[:, :KV_LORA].astype(BF16), _row(kv_norm_g), wk, wv, wpe,
                        cosk, sink, q_w_down[0].astype(BF16), _row(q_norm_g[0]), wq, cosq, sinq)
    o = _attention(q, k, v)
    h = _ffn(h, *_ffn_params(f_w_in[1], f_conv_w[1], f_conv_b[1], f_w_out[1]),
             _row(ln_g[1, 1]), _row(ln_b[1, 1]),
             attn=(o, o_w[0].astype(BF16), _row(ln_g[1, 0]), _row(ln_b[1, 0])))
    return h[:, PAD + N_META:]
```

```python
import functools
import math

import jax
import jax.numpy as jnp
from jax import lax
from jax.experimental import pallas as pl
from jax.experimental.pallas import tpu as pltpu

F32 = jnp.float32
BF16 = jnp.bfloat16

D_MODEL = 1024
N_META = 16
CHUNK = 64
DEPTH = 2
DN_ALPHA = (2.0 * DEPTH) ** 0.25
LN_EPS = 1e-5
RMS_EPS = 1e-6
LOG2_E = math.log2(math.e)

SSM_D_INNER = 2 * D_MODEL
SSM_HEAD_DIM = 64
SSM_HEADS = SSM_D_INNER // SSM_HEAD_DIM
SSM_GROUPS = 4
SSM_STATE = 128
SSM_CONV = 4
SSM_BC = SSM_GROUPS * SSM_STATE
SSM_CONV_DIM = SSM_D_INNER + 2 * SSM_BC
GROUP_COLS = SSM_D_INNER // SSM_GROUPS
HEAD_LANES = 128
FRONT_CHUNK = 512
FRONT_BUFFERS = 3

MLA_HEADS = 16
Q_LORA = 384
KV_LORA = 256
QK_NOPE = 64
QK_ROPE = 32
V_HEAD = 64
ROPE_THETA = 10000.0
HEAD_PAD = 128
ATT_HEADS = 8

D_FF = 2816
FFN_CONV = 3
FF_CHUNK = 256
N_FF_CHUNKS = D_FF // FF_CHUNK
FF_AHEAD = 2
SUBLANES = 8
LANES = 128

PAD = 112
PREFIX = PAD + N_META
ROW_TILE = 384
HALO = 8
NEG_BIG = -1e30
VMEM_LIMIT = 60 * 1024 * 1024


def _const_spec(shape):
    nd = len(shape)
    return pl.BlockSpec(shape, lambda *_: (0,) * nd, pipeline_mode=pl.Buffered(1))


def _layer_norm(v, g, b):
    mu = jnp.mean(v, axis=-1, keepdims=True)
    d = v - mu
    var = jnp.mean(d * d, axis=-1, keepdims=True)
    return d * lax.rsqrt(var + LN_EPS) * g + b


def _rms(v):
    return v * lax.rsqrt(jnp.mean(v * v, axis=-1, keepdims=True) + RMS_EPS)


def _silu(v):
    return v / (1.0 + jnp.exp(-v))


def _split_bf16(v, parts):
    out = []
    rem = v
    for _ in range(parts):
        p = rem.astype(BF16)
        out.append(p)
        rem = rem - p.astype(F32)
    return out


def _dot(a, b):
    return jnp.dot(a, b, preferred_element_type=F32)


def _dot_split(lhs_parts, rhs):
    acc = _dot(lhs_parts[0], rhs)
    for p in lhs_parts[1:]:
        acc = acc + _dot(p, rhs)
    return acc


def _sublane_major(perm_ref, v):
    rows, width = v.shape
    stride = rows // SUBLANES
    n = width // LANES
    for s in range(n):
        perm_ref[s] = v[:, s * LANES:(s + 1) * LANES]
    return jnp.concatenate(
        [jnp.concatenate([perm_ref[s, pl.ds(a, SUBLANES, stride=stride), :]
                          for s in range(n)], axis=1) for a in range(stride)], axis=0)


def _natural_order(perm_ref, v):
    rows, width = v.shape
    stride = rows // SUBLANES
    n = width // LANES
    for a in range(stride):
        for s in range(n):
            perm_ref[s, pl.ds(a, SUBLANES, stride=stride), :] = (
                v[a * SUBLANES:(a + 1) * SUBLANES, s * LANES:(s + 1) * LANES])
    return jnp.concatenate([perm_ref[s] for s in range(n)], axis=1)


def _matmul_rows(h, t):
    rows, width = h.shape
    p_i = lax.broadcasted_iota(jnp.int32, (rows, LANES), 0)
    time = (p_i & (SUBLANES - 1)) * (rows // SUBLANES) + (p_i >> 3)
    keep = time >= jnp.where(t > 0, 0, PAD)
    return jnp.concatenate([jnp.where(keep, h[:, s * LANES:(s + 1) * LANES], 0.0)
                            for s in range(width // LANES)], axis=1).astype(BF16)


def _causal_conv(x, halo_ref, cw_ref, cb_ref, cs):
    rows, width = x.shape
    taps = cw_ref.shape[0]
    first_sublane = lax.broadcasted_iota(jnp.int32, (SUBLANES, width), 0) == 0
    tail = halo_ref[:, cs]
    halo_ref[:, cs] = x[rows - (taps - 1) * SUBLANES:, :]
    wrapped = []
    for j in range(1, taps):
        cur = x[rows - j * SUBLANES:rows - (j - 1) * SUBLANES, :]
        prev = tail[(taps - 1 - j) * SUBLANES:(taps - j) * SUBLANES, :]
        wrapped.append(jnp.where(first_sublane, pltpu.roll(prev, 1, 0), pltpu.roll(cur, 1, 0)))
    out = cb_ref[:, cs] + cw_ref[taps - 1:taps, cs] * x
    for j in range(1, taps):
        shifted = jnp.concatenate(wrapped[j - 1::-1] + [x[:rows - j * SUBLANES, :]], axis=0)
        out = out + cw_ref[taps - 1 - j:taps - j, cs] * shifted
    return out


def _input_rows(x_refs, prefix_ref, t):
    first = jnp.where(t > 0, x_refs[0][0], prefix_ref[...])
    return jnp.concatenate([first] + [r[0] for r in x_refs[1:]], axis=0)


def _input_specs():
    n = ROW_TILE // PREFIX
    return [pl.BlockSpec((1, PREFIX, D_MODEL),
                         lambda b, t, k=k: (b, jnp.maximum(n * t - 1 + k, 0), 0))
            for k in range(n)] + [_const_spec((PREFIX, D_MODEL))]


def _mamba_front_kernel(xa_ref, xb_ref, xc_ref, prefix_ref,
                        wz_ref, wxbc_ref, wdt_ref, cw_ref, cb_ref, dtb_ref, alog_ref,
                        z_ref, xbc_ref, dt_ref, da_ref, halo_ref, buf_ref):
    t = pl.program_id(1)
    rows = ROW_TILE
    width = FRONT_CHUNK
    n_x = SSM_CONV_DIM // width
    n_z = SSM_D_INNER // width

    @pl.when(t == 0)
    def _():
        halo_ref[...] = jnp.zeros(halo_ref.shape, F32)

    u = _input_rows((xa_ref, xb_ref, xc_ref), prefix_ref, t).astype(BF16)

    def project(c):
        cs = slice(c * width, (c + 1) * width)
        slot = c % FRONT_BUFFERS
        buf_ref[slot, 0:HALO, :] = halo_ref[:, cs]
        pre = _dot(u, wxbc_ref[:, cs])
        buf_ref[slot, HALO:HALO + rows, :] = pre
        buf_ref[slot, HALO:HALO + PAD, :] = jnp.where(t > 0, pre[:PAD], 0.0)

    def gate(c):
        cs = slice(c * width, (c + 1) * width)
        z_ref[0, :, cs] = _dot(u, wz_ref[:, cs]).astype(BF16)

    for c in range(FRONT_BUFFERS - 1):
        project(c)
    for c in range(n_x):
        if c + FRONT_BUFFERS - 1 < n_x:
            project(c + FRONT_BUFFERS - 1)
        if c < n_z:
            gate(c)
        cs = slice(c * width, (c + 1) * width)
        slot = c % FRONT_BUFFERS
        acc = cb_ref[:, cs]
        for k in range(SSM_CONV):
            off = HALO - (SSM_CONV - 1) + k
            acc = acc + cw_ref[k:k + 1, cs] * buf_ref[slot, off:off + rows, :]
        halo_ref[:, cs] = buf_ref[slot, rows:rows + HALO, :]
        act = _silu(acc).astype(BF16)
        xbc_ref[0, :, cs] = act
        xbc_ref[0, 0:PAD, cs] = jnp.where(t > 0, act[:PAD], jnp.zeros((), BF16))

    x = _dot(u, wdt_ref[...]) + dtb_ref[...]
    dt = jnp.maximum(x, 0.0) + jnp.log1p(jnp.exp(-jnp.abs(x)))
    da = dt * (-jnp.exp(alog_ref[...]) * LOG2_E)
    dt_ref[0] = dt
    da_ref[0] = da
    dt_ref[0, 0:PAD, :] = jnp.where(t > 0, dt[:PAD], 0.0)
    da_ref[0, 0:PAD, :] = jnp.where(t > 0, da[:PAD], 0.0)


def _mamba_front(x, prefix, wz, wxbc, wdt, cw, cb, dtb, alog):
    bsz, seq, _ = x.shape
    lp = PREFIX + seq
    nt = lp // ROW_TILE
    row = lambda w: pl.BlockSpec((1, ROW_TILE, w), lambda b, t: (b, t, 0))
    return pl.pallas_call(
        _mamba_front_kernel,
        grid=(bsz, nt),
        in_specs=_input_specs() + [_const_spec(wz.shape), _const_spec(wxbc.shape),
                  _const_spec(wdt.shape), _const_spec(cw.shape), _const_spec(cb.shape),
                  _const_spec(dtb.shape), _const_spec(alog.shape)],
        out_specs=[row(SSM_D_INNER), row(SSM_CONV_DIM), row(HEAD_LANES), row(HEAD_LANES)],
        out_shape=[jax.ShapeDtypeStruct((bsz, lp, SSM_D_INNER), BF16),
                   jax.ShapeDtypeStruct((bsz, lp, SSM_CONV_DIM), BF16),
                   jax.ShapeDtypeStruct((bsz, lp, HEAD_LANES), F32),
                   jax.ShapeDtypeStruct((bsz, lp, HEAD_LANES), F32)],
        scratch_shapes=[pltpu.VMEM((HALO, SSM_CONV_DIM), F32),
                        pltpu.VMEM((FRONT_BUFFERS, ROW_TILE + HALO, FRONT_CHUNK), F32)],
        compiler_params=pltpu.CompilerParams(
            dimension_semantics=("arbitrary", "arbitrary"), vmem_limit_bytes=VMEM_LIMIT),
        name="mamba_front",
    )(x, x, x, prefix, wz, wxbc, wdt, cw, cb, dtb, alog)


def _mamba_scan_kernel(xa_ref, xb_ref, xc_ref, prefix_ref,
                       z_ref, xs_ref, b_ref, c_ref, dt_ref, da_ref, e_ref, dskip_ref,
                       ng_ref, wout_ref, lng_ref, lnb_ref, out_ref,
                       acs_ref, dte_ref, y_ref, state_ref):
    t = pl.program_id(1)
    rows = ROW_TILE
    n_chunks = rows // CHUNK

    @pl.when(t == 0)
    def _():
        state_ref[...] = jnp.zeros(state_ref.shape, F32)

    r_i = lax.broadcasted_iota(jnp.int32, (rows, rows), 0)
    c_i = lax.broadcasted_iota(jnp.int32, (rows, rows), 1)
    tri = jnp.where(((r_i >> 6) == (c_i >> 6)) & (c_i <= r_i), 1.0, 0.0).astype(BF16)
    acs = sum(_dot(tri, p) for p in _split_bf16(da_ref[0], 3))
    lane_copy = lax.broadcasted_iota(jnp.int32, (rows, HEAD_LANES), 1) // SSM_HEADS

    def packed_terms(v, parts):
        terms = _split_bf16(v, parts)
        out = jnp.zeros((rows, HEAD_LANES), BF16)
        for k, term in enumerate(terms):
            out = jnp.where(lane_copy == k, term, out)
        return out

    acs_ref[...] = _dot(packed_terms(acs, 3), e_ref[...])
    dte_ref[...] = _dot(packed_terms(dt_ref[0], 2), e_ref[...])

    s_i = lax.broadcasted_iota(jnp.int32, (CHUNK, 2 * CHUNK), 0)
    l_i = lax.broadcasted_iota(jnp.int32, (CHUNK, 2 * CHUNK), 1)
    diag = (l_i & (CHUNK - 1)) == s_i
    causal = (l_i & (CHUNK - 1)) <= s_i
    br_i = lax.broadcasted_iota(jnp.int32, (2 * CHUNK, 2 * CHUNK), 0)
    bc_i = lax.broadcasted_iota(jnp.int32, (2 * CHUNK, 2 * CHUNK), 1)
    blockdiag = (br_i >> 6) == (bc_i >> 6)

    def chunk_body(c, carry):
        r0 = pl.multiple_of(c * CHUNK, CHUNK)
        acs_c = acs_ref[pl.ds(r0, CHUNK), :]
        dte_c = dte_ref[pl.ds(r0, CHUNK), :]
        xs_c = xs_ref[0, pl.ds(r0, CHUNK), :]
        b_c = b_ref[0, pl.ds(r0, CHUNK), :]
        c_c = c_ref[0, pl.ds(r0, CHUNK), :]
        last = acs_c[CHUNK - 1:CHUNK, :]
        groups = [slice(g * SSM_STATE, (g + 1) * SSM_STATE) for g in range(SSM_GROUPS)]
        gcols = [slice(g * GROUP_COLS, (g + 1) * GROUP_COLS) for g in range(SSM_GROUPS)]
        pair = 2 * SSM_HEAD_DIM
        states = [state_ref[g] for g in range(SSM_GROUPS)]
        cb2 = [lax.dot_general(c_c[:, gl], jnp.concatenate([b_c[:, gl], b_c[:, gl]], axis=0),
                               (((1,), (1,)), ((), ())), preferred_element_type=F32)
               for gl in groups]
        y_off = [_dot(c_c[:, gl], st.astype(BF16)) for gl, st in zip(groups, states)]
        x_dt = xs_c.astype(F32) * dte_c
        xw = (x_dt * jnp.exp2(last - acs_c)).astype(BF16)
        x_dt = x_dt.astype(BF16)
        s_new = [lax.dot_general(b_c[:, gl], xw[:, gc], (((0,), (0,)), ((), ())),
                                 preferred_element_type=F32) for gl, gc in zip(groups, gcols)]
        grow = jnp.exp2(acs_c)
        chunk_decay = jnp.exp2(last)
        for g in range(SSM_GROUPS):
            for jj in range(GROUP_COLS // pair):
                pl_ = slice(g * GROUP_COLS + jj * pair, g * GROUP_COLS + (jj + 1) * pair)
                a_p = acs_c[:, pl_]
                a_row = jnp.sum(jnp.where(diag, a_p, 0.0), axis=0, keepdims=True)
                decay = jnp.exp2(jnp.where(causal, a_p - a_row, NEG_BIG))
                m2 = (cb2[g] * decay).astype(BF16)
                x_p = x_dt[:, pl_]
                x_bd = jnp.where(blockdiag, jnp.concatenate([x_p, x_p], axis=0),
                                 jnp.zeros((), BF16))
                y_ref[pl.ds(r0, CHUNK), pl_] = (
                    _dot(m2, x_bd) + y_off[g][:, jj * pair:(jj + 1) * pair] * grow[:, pl_])
            state_ref[g] = states[g] * chunk_decay[:, gcols[g]] + s_new[g]
        return carry

    lax.fori_loop(0, n_chunks, chunk_body, 0)

    mix = jnp.zeros((rows, D_MODEL), F32)
    for g in range(SSM_GROUPS):
        gc = slice(g * GROUP_COLS, (g + 1) * GROUP_COLS)
        y = y_ref[:, gc] + xs_ref[0, :, gc].astype(F32) * dskip_ref[:, gc]
        yn = _rms(y * _silu(z_ref[0, :, gc].astype(F32))) * ng_ref[:, gc]
        mix = mix + _dot(yn.astype(BF16), wout_ref[gc, :])
    h = _input_rows((xa_ref, xb_ref, xc_ref), prefix_ref, t)
    out_ref[0] = _layer_norm(DN_ALPHA * h + mix, lng_ref[...], lnb_ref[...])


def _mamba_scan(x, prefix, z, xbc, dt, da, e, dskip, ng, wout, lng, lnb):
    bsz, seq, _ = x.shape
    lp = PREFIX + seq
    nt = lp // ROW_TILE
    row = lambda w, cb=0: pl.BlockSpec((1, ROW_TILE, w), lambda b, t: (b, t, cb))
    return pl.pallas_call(
        _mamba_scan_kernel,
        grid=(bsz, nt),
        in_specs=_input_specs() + [row(SSM_D_INNER), row(SSM_D_INNER),
                  row(SSM_BC, SSM_D_INNER // SSM_BC), row(SSM_BC, SSM_D_INNER // SSM_BC + 1),
                  row(HEAD_LANES), row(HEAD_LANES),
                  _const_spec(e.shape), _const_spec(dskip.shape), _const_spec(ng.shape),
                  _const_spec(wout.shape), _const_spec(lng.shape), _const_spec(lnb.shape)],
        out_specs=row(D_MODEL),
        out_shape=jax.ShapeDtypeStruct((bsz, lp, D_MODEL), F32),
        scratch_shapes=[pltpu.VMEM((ROW_TILE, SSM_D_INNER), F32),
                        pltpu.VMEM((ROW_TILE, SSM_D_INNER), F32),
                        pltpu.VMEM((ROW_TILE, SSM_D_INNER), F32),
                        pltpu.VMEM((SSM_GROUPS, SSM_STATE, GROUP_COLS), F32)],
        compiler_params=pltpu.CompilerParams(
            dimension_semantics=("arbitrary", "arbitrary"), vmem_limit_bytes=VMEM_LIMIT),
        name="mamba_scan",
    )(x, x, x, prefix, z, xbc, xbc, xbc, dt, da, e, dskip, ng, wout, lng, lnb)


def _ffn_body(h, t, win_ref, cw_ref, cb_ref, wout_ref, lng_ref, lnb_ref, out_ref,
              halo_ref, perm_ref):
    rows = ROW_TILE

    @pl.when(t == 0)
    def _():
        halo_ref[...] = jnp.zeros(halo_ref.shape, F32)

    h = _sublane_major(perm_ref, h)
    u = _matmul_rows(h, t)

    def cols(cb):
        return (slice(cb * FF_CHUNK, (cb + 1) * FF_CHUNK),
                slice(D_FF + cb * FF_CHUNK, D_FF + (cb + 1) * FF_CHUNK))

    def conv(x, cs):
        return _causal_conv(x, halo_ref, cw_ref, cb_ref, cs)

    def project(cb):
        return [_dot(u, win_ref[:, cs]) for cs in cols(cb)]

    pre = {cb: project(cb) for cb in range(FF_AHEAD)}
    acc = jnp.zeros((rows, D_MODEL), F32)
    for cb in range(N_FF_CHUNKS):
        if cb + FF_AHEAD < N_FF_CHUNKS:
            pre[cb + FF_AHEAD] = project(cb + FF_AHEAD)
        gate, up = pre.pop(cb)
        gate_cols, up_cols = cols(cb)
        act = (_silu(conv(gate, gate_cols)) * conv(up, up_cols)).astype(BF16)
        acc = acc + _dot(act, wout_ref[cb * FF_CHUNK:(cb + 1) * FF_CHUNK, :])
    out = _layer_norm(DN_ALPHA * h + acc, lng_ref[...], lnb_ref[...])
    out_ref[0] = _natural_order(perm_ref, out)


def _ffn_kernel(h_ref, win_ref, cw_ref, cb_ref, wout_ref, lng_ref, lnb_ref, out_ref,
                halo_ref, perm_ref):
    _ffn_body(h_ref[0], pl.program_id(1), win_ref, cw_ref, cb_ref, wout_ref, lng_ref, lnb_ref,
              out_ref, halo_ref, perm_ref)


def _attn_out_ffn_kernel(h_ref, o_ref, wo_ref, lng0_ref, lnb0_ref, win_ref, cw_ref, cb_ref,
                         wout_ref, lng_ref, lnb_ref, out_ref, halo_ref, perm_ref):
    mix = _dot(o_ref[0], wo_ref[...])
    h = _layer_norm(DN_ALPHA * h_ref[0] + mix, lng0_ref[...], lnb0_ref[...])
    _ffn_body(h, pl.program_id(1), win_ref, cw_ref, cb_ref, wout_ref, lng_ref, lnb_ref,
              out_ref, halo_ref, perm_ref)


def _ffn(h, win, cw, cb, wout, lng, lnb, attn=None):
    bsz, lp, _ = h.shape
    nt = lp // ROW_TILE
    row = pl.BlockSpec((1, ROW_TILE, D_MODEL), lambda b, t: (b, t, 0))
    consts = (win, cw, cb, wout, lng, lnb)
    if attn is None:
        kern, args, specs = _ffn_kernel, (h,) + consts, [row]
    else:
        o, wo, lng0, lnb0 = attn
        kern = _attn_out_ffn_kernel
        args = (h, o, wo, lng0, lnb0) + consts
        specs = [row, row, _const_spec(wo.shape), _const_spec(lng0.shape),
                 _const_spec(lnb0.shape)]
    return pl.pallas_call(
        kern,
        grid=(bsz, nt),
        in_specs=specs + [_const_spec(a.shape) for a in consts],
        out_specs=row,
        out_shape=jax.ShapeDtypeStruct((bsz, lp, D_MODEL), F32),
        scratch_shapes=[pltpu.VMEM((2 * SUBLANES, 2 * D_FF), F32),
                        pltpu.VMEM((D_MODEL // LANES, ROW_TILE, LANES), F32)],
        compiler_params=pltpu.CompilerParams(
            dimension_semantics=("arbitrary", "arbitrary"), vmem_limit_bytes=VMEM_LIMIT),
        name="conv_ffn" if attn is None else "attn_out_conv_ffn",
    )(*args)


def _mla_proj_kernel(h_ref, wkvd_ref, kvg_ref, wk_ref, wv_ref, wpe_ref, cosk_ref,
                     sink_ref, wqd_ref, qg_ref, wq_ref, cosq_ref, sinq_ref,
                     q_ref, k_ref, v_ref):
    half = QK_ROPE // 2
    lane = lax.broadcasted_iota(jnp.int32, (1, HEAD_PAD), 1)
    first_half = lane < QK_NOPE + half

    def rope(x, cos, sin_signed):
        swapped = jnp.where(first_half, pltpu.roll(x, HEAD_PAD - half, 1),
                            pltpu.roll(x, half, 1))
        return x * cos + swapped * sin_signed

    u = h_ref[0].astype(BF16)
    cq = (_rms(_dot(u, wqd_ref[...])) * qg_ref[...]).astype(BF16)
    q_all = _dot(cq, wq_ref[...])
    k_pe = rope(_dot(u, wpe_ref[...]), cosk_ref[...], sink_ref[...])
    ckv = (_rms(_dot(u, wkvd_ref[...])) * kvg_ref[...]).astype(BF16)
    k_all = _dot(ckv, wk_ref[...])
    v_all = _dot(ckv, wv_ref[...])
    ones_col = jnp.where(lane == V_HEAD, 1.0, 0.0)
    cosq = cosq_ref[...]
    sinq = sinq_ref[...]
    for hd in range(MLA_HEADS):
        hl = slice(hd * HEAD_PAD, (hd + 1) * HEAD_PAD)
        q_ref[0, hd] = rope(q_all[:, hl], cosq, sinq).astype(BF16)
    for hd in range(MLA_HEADS):
        hl = slice(hd * HEAD_PAD, (hd + 1) * HEAD_PAD)
        k_ref[0, hd] = (k_all[:, hl] + k_pe).astype(BF16)
        v_ref[0, hd] = (v_all[:, hl] + ones_col).astype(BF16)


def _mla_proj(h, wkvd, kvg, wk, wv, wpe, cosk, sink, wqd, qg, wq, cosq, sinq):
    bsz, lp, _ = h.shape
    nt = lp // ROW_TILE
    row = pl.BlockSpec((1, ROW_TILE, D_MODEL), lambda b, t: (b, t, 0))
    tab = pl.BlockSpec((ROW_TILE, HEAD_PAD), lambda b, t: (t, 0))
    head = pl.BlockSpec((1, MLA_HEADS, ROW_TILE, HEAD_PAD), lambda b, t: (b, 0, t, 0))
    shape = jax.ShapeDtypeStruct((bsz, MLA_HEADS, lp, HEAD_PAD), BF16)
    c = _const_spec
    return pl.pallas_call(
        _mla_proj_kernel,
        grid=(bsz, nt),
        in_specs=[row, c(wkvd.shape), c(kvg.shape), c(wk.shape), c(wv.shape), c(wpe.shape),
                  tab, tab, c(wqd.shape), c(qg.shape), c(wq.shape), tab, tab],
        out_specs=[head, head, head],
        out_shape=[shape, shape, shape],
        compiler_params=pltpu.CompilerParams(
            dimension_semantics=("arbitrary", "arbitrary"), vmem_limit_bytes=VMEM_LIMIT),
        name="mla_proj",
    )(h, wkvd, kvg, wk, wv, wpe, cosk, sink, wqd, qg, wq, cosq, sinq)


def _attention_kernel(q_ref, k_ref, v_ref, o_ref, m_ref, acc_ref):
    i = pl.program_id(2)
    rows = ROW_TILE
    blk = LANES
    n_blk = rows // blk
    shift = CHUNK.bit_length() - 1
    col = lax.broadcasted_iota(jnp.int32, (rows, blk), 1)
    b_row = lax.broadcasted_iota(jnp.int32, (blk, blk), 0)
    b_col = lax.broadcasted_iota(jnp.int32, (blk, blk), 1)
    causal_blk = (b_col >> shift) <= (b_row >> shift)
    qs = [q_ref[0, hh] for hh in range(ATT_HEADS)]

    def real_keys(j, diag):
        return (b_col if diag else col) + j * rows >= PAD

    def scores(j, hh, valid0, diag):
        k0 = pl.multiple_of(j * rows, rows)
        s = lax.dot_general(qs[hh], k_ref[0, hh, pl.ds(k0, rows), :],
                            (((1,), (1,)), ((), ())), preferred_element_type=F32)
        if not diag:
            return jnp.concatenate([jnp.where(valid0, s[:, :blk], NEG_BIG), s[:, blk:]], axis=1)
        out = []
        for rb in range(n_blk):
            r_s = slice(rb * blk, (rb + 1) * blk)
            row = []
            for cb in range(n_blk):
                if cb > rb:
                    row.append(jnp.full((blk, blk), NEG_BIG, F32))
                    continue
                sb = s[r_s, cb * blk:(cb + 1) * blk]
                ok = causal_blk if cb == rb else None
                if cb == 0:
                    ok = valid0 if ok is None else ok & valid0
                row.append(sb if ok is None else jnp.where(ok, sb, NEG_BIG))
            out.append(jnp.concatenate(row, axis=1))
        return jnp.concatenate(out, axis=0)

    def absorb(j, hh, s):
        k0 = pl.multiple_of(j * rows, rows)
        m = m_ref[hh]
        m_new = jnp.maximum(m, jnp.max(s, axis=-1, keepdims=True))
        p = jnp.exp2(s - jnp.concatenate([m_new] * n_blk, axis=1)).astype(BF16)
        m_ref[hh] = m_new
        acc_ref[hh] = (jnp.exp2(m - m_new) * acc_ref[hh]
                       + _dot(p, v_ref[0, hh, pl.ds(k0, rows), :]))

    def steps(tiles, diag=False):
        valid = [real_keys(j, diag) for j in tiles]
        cur = [scores(tiles[0], hh, valid[0], diag) for hh in range(ATT_HEADS)]
        for n, j in enumerate(tiles):
            nxt = []
            for hh in range(ATT_HEADS):
                if n + 1 < len(tiles):
                    nxt.append(scores(tiles[n + 1], hh, valid[n + 1], diag))
                absorb(j, hh, cur[hh])
            cur = nxt

    m_ref[...] = jnp.full(m_ref.shape, NEG_BIG, F32)
    acc_ref[...] = jnp.zeros(acc_ref.shape, F32)
    start = 0
    for width, trips in ((1, i & 1), (2, (i >> 1) & 1), (4, i >> 2)):
        def group(g, _, start=start, width=width):
            steps([start + width * g + w for w in range(width)])
            return 0
        lax.fori_loop(0, trips, group, 0)
        start = start + width * trips
    steps([i], diag=True)
    outs = []
    for hh in range(ATT_HEADS):
        acc = acc_ref[hh]
        outs.append(acc[:, :V_HEAD] / acc[:, V_HEAD:V_HEAD + 1])
    o_ref[0] = jnp.concatenate(outs, axis=1).astype(BF16)


def _attention(q, k, v):
    bsz, nh, lp, _ = q.shape
    nt = lp // ROW_TILE
    hs = ATT_HEADS
    return pl.pallas_call(
        _attention_kernel,
        grid=(bsz, nh // hs, nt),
        in_specs=[pl.BlockSpec((1, hs, ROW_TILE, HEAD_PAD), lambda b, hp, i: (b, hp, i, 0)),
                  pl.BlockSpec((1, hs, lp, HEAD_PAD), lambda b, hp, i: (b, hp, 0, 0)),
                  pl.BlockSpec((1, hs, lp, HEAD_PAD), lambda b, hp, i: (b, hp, 0, 0))],
        out_specs=pl.BlockSpec((1, ROW_TILE, hs * V_HEAD), lambda b, hp, i: (b, i, hp)),
        out_shape=jax.ShapeDtypeStruct((bsz, lp, MLA_HEADS * V_HEAD), BF16),
        scratch_shapes=[pltpu.VMEM((hs, ROW_TILE, LANES), F32),
                        pltpu.VMEM((hs, ROW_TILE, HEAD_PAD), F32)],
        compiler_params=pltpu.CompilerParams(
            dimension_semantics=("arbitrary", "arbitrary", "arbitrary"),
            vmem_limit_bytes=VMEM_LIMIT),
        name="mla_attention",
    )(q, k, v)


def _row(v, width=None):
    v = v.astype(F32).reshape(1, -1)
    if width is not None and v.shape[1] < width:
        v = jnp.pad(v, ((0, 0), (0, width - v.shape[1])))
    return v


def _ffn_params(w_in, conv_w, conv_b, w_out):
    return w_in.astype(BF16), conv_w.astype(F32), _row(conv_b), w_out.astype(BF16)


def _rope_tables(lp):
    half = QK_ROPE // 2
    pos = jnp.maximum(jnp.arange(lp, dtype=F32) - PAD, 0.0)
    inv_freq = ROPE_THETA ** (-jnp.arange(half, dtype=F32) * 2.0 / QK_ROPE)
    ang = pos[:, None] * inv_freq[None, :]
    cos, sin = jnp.cos(ang), jnp.sin(ang)
    zeros = lambda n: jnp.zeros((lp, n), F32)
    tail = HEAD_PAD - QK_NOPE - QK_ROPE
    cosk = jnp.concatenate([zeros(QK_NOPE), cos, cos, zeros(tail)], axis=1)
    sink = jnp.concatenate([zeros(QK_NOPE), -sin, sin, zeros(tail)], axis=1)
    c = (QK_NOPE + QK_ROPE) ** -0.5 * LOG2_E
    cosq = c * jnp.concatenate([jnp.ones((lp, QK_NOPE), F32), cos, cos, zeros(tail)], axis=1)
    return cosk, sink, cosq, c * sink


def _pad_last(a, before, total):
    pads = [(0, 0)] * (a.ndim - 1) + [(before, total - before - a.shape[-1])]
    return jnp.pad(a, pads)


def kernel(x, meta_tokens, m_w_in, m_conv_w, m_conv_b, m_dt_bias, m_a_log, m_d, m_norm_g, m_w_out, kv_w_down, kv_norm_g, kv_w_up, q_w_down, q_norm_g, q_w_up, o_w, f_w_in, f_conv_w, f_conv_b, f_w_out, ln_g, ln_b):
    bsz, seq, _ = x.shape
    lp = PAD + N_META + seq
    assert lp % ROW_TILE == 0 and PREFIX % CHUNK == 0 and ROW_TILE % PREFIX == 0
    assert seq % PREFIX == 0
    prefix = jnp.concatenate([jnp.zeros((PAD, D_MODEL), x.dtype), meta_tokens.astype(x.dtype)],
                             axis=0)

    w_in = m_w_in[0]
    wz = w_in[:, :SSM_D_INNER].astype(BF16)
    wxbc = w_in[:, SSM_D_INNER:SSM_D_INNER + SSM_CONV_DIM].astype(BF16)
    copies = HEAD_LANES // SSM_HEADS
    wdt = jnp.tile(w_in[:, SSM_D_INNER + SSM_CONV_DIM:], (1, copies)).astype(BF16)
    z, xbc, dt, da = _mamba_front(x, prefix, wz, wxbc, wdt, m_conv_w[0].astype(F32),
                                  _row(m_conv_b[0]), _row(jnp.tile(m_dt_bias[0], copies)),
                                  _row(jnp.tile(m_a_log[0], copies)))
    head_of_col = jnp.arange(SSM_D_INNER) // SSM_HEAD_DIM
    expand = (jnp.arange(HEAD_LANES)[:, None] % SSM_HEADS == head_of_col[None, :]).astype(BF16)
    h = _mamba_scan(x, prefix, z, xbc, dt, da, expand, _row(jnp.repeat(m_d[0], SSM_HEAD_DIM)),
                    _row(m_norm_g[0]), m_w_out[0].astype(BF16), _row(ln_g[0, 0]), _row(ln_b[0, 0]))
    h = _ffn(h, *_ffn_params(f_w_in[0], f_conv_w[0], f_conv_b[0], f_w_out[0]),
             _row(ln_g[0, 1]), _row(ln_b[0, 1]))

    cosk, sink, cosq, sinq = _rope_tables(lp)
    w_rope = kv_w_down[:, KV_LORA:]
    wpe = _pad_last(w_rope, QK_NOPE, HEAD_PAD).astype(BF16)
    kv_up = kv_w_up.reshape(KV_LORA, MLA_HEADS, QK_NOPE + V_HEAD)
    wk = _pad_last(kv_up[..., :QK_NOPE], 0, HEAD_PAD).reshape(KV_LORA, -1).astype(BF16)
    wv = _pad_last(kv_up[..., QK_NOPE:], 0, HEAD_PAD).reshape(KV_LORA, -1).astype(BF16)
    q_up = q_w_up[0].reshape(Q_LORA, MLA_HEADS, QK_NOPE + QK_ROPE)
    wq = _pad_last(q_up, 0, HEAD_PAD).reshape(Q_LORA, -1).astype(BF16)
    q, k, v = _mla_proj(h, kv_w_down[:, :KV_LORA].astype(BF16), _row(kv_norm_g), wk, wv, wpe,
                        cosk, sink, q_w_down[0].astype(BF16), _row(q_norm_g[0]), wq, cosq, sinq)
    o = _attention(q, k, v)
    h = _ffn(h, *_ffn_params(f_w_in[1], f_conv_w[1], f_conv_b[1], f_w_out[1]),
             _row(ln_g[1, 1]), _row(ln_b[1, 1]),
             attn=(o, o_w[0].astype(BF16), _row(ln_g[1, 0]), _row(ln_b[1, 0])))
    return h[:, PAD + N_META:]
```

```python
import math

import jax
import jax.numpy as jnp
from jax import lax
from jax.experimental import pallas as pl
from jax.experimental.pallas import tpu as pltpu

F32 = jnp.float32
BF16 = jnp.bfloat16

D_MODEL = 1024
N_META = 16
CHUNK = 64
DEPTH = 2
DN_ALPHA = (2.0 * DEPTH) ** 0.25
LN_EPS = 1e-5
RMS_EPS = 1e-6
LOG2_E = math.log2(math.e)

SSM_D_INNER = 2 * D_MODEL
SSM_HEAD_DIM = 64
SSM_HEADS = SSM_D_INNER // SSM_HEAD_DIM
SSM_GROUPS = 4
SSM_STATE = 128
SSM_CONV = 4
SSM_BC = SSM_GROUPS * SSM_STATE
SSM_CONV_DIM = SSM_D_INNER + 2 * SSM_BC
GROUP_COLS = SSM_D_INNER // SSM_GROUPS
HEAD_LANES = 128
FRONT_CHUNK = 512
FRONT_BUFFERS = 3

MLA_HEADS = 16
Q_LORA = 384
KV_LORA = 256
QK_NOPE = 64
QK_ROPE = 32
V_HEAD = 64
ROPE_THETA = 10000.0
HEAD_PAD = 128
ATT_HEADS = 8

D_FF = 2816
FFN_CONV = 3
FF_CHUNK = 256
N_FF_CHUNKS = D_FF // FF_CHUNK
FF_AHEAD = 3
SUBLANES = 8
LANES = 128

PAD = 112
PREFIX = PAD + N_META
ROW_TILE = 384
HALO = 8
NEG_BIG = -1e30
VMEM_LIMIT = 60 * 1024 * 1024


def _const_spec(shape):
    nd = len(shape)
    return pl.BlockSpec(shape, lambda *_: (0,) * nd, pipeline_mode=pl.Buffered(1))


def _layer_norm(v, g, b):
    mu = jnp.mean(v, axis=-1, keepdims=True)
    d = v - mu
    var = jnp.mean(d * d, axis=-1, keepdims=True)
    return d * lax.rsqrt(var + LN_EPS) * g + b


def _rms(v):
    return v * lax.rsqrt(jnp.mean(v * v, axis=-1, keepdims=True) + RMS_EPS)


def _silu(v):
    return v / (1.0 + jnp.exp(-v))


def _split_bf16(v, parts):
    out = []
    rem = v
    for _ in range(parts):
        p = rem.astype(BF16)
        out.append(p)
        rem = rem - p.astype(F32)
    return out


def _dot(a, b):
    return jnp.dot(a, b, preferred_element_type=F32)


def _packed_terms(v, parts):
    lane_copy = lax.broadcasted_iota(jnp.int32, v.shape, 1) // SSM_HEADS
    out = jnp.zeros(v.shape, BF16)
    for k, term in enumerate(_split_bf16(v, parts)):
        out = jnp.where(lane_copy == k, term, out)
    return out


def _sublane_major(perm_ref, v):
    rows, width = v.shape
    stride = rows // SUBLANES
    n = width // LANES
    for s in range(n):
        perm_ref[s] = v[:, s * LANES:(s + 1) * LANES]
    return jnp.concatenate(
        [jnp.concatenate([perm_ref[s, pl.ds(a, SUBLANES, stride=stride), :]
                          for s in range(n)], axis=1) for a in range(stride)], axis=0)


def _natural_order(perm_ref, v):
    rows, width = v.shape
    stride = rows // SUBLANES
    n = width // LANES
    for a in range(stride):
        for s in range(n):
            perm_ref[s, pl.ds(a, SUBLANES, stride=stride), :] = (
                v[a * SUBLANES:(a + 1) * SUBLANES, s * LANES:(s + 1) * LANES])
    return jnp.concatenate([perm_ref[s] for s in range(n)], axis=1)


def _matmul_rows(h, t):
    rows, width = h.shape
    p_i = lax.broadcasted_iota(jnp.int32, (rows, LANES), 0)
    time = (p_i % SUBLANES) * (rows // SUBLANES) + p_i // SUBLANES
    keep = time >= jnp.where(t > 0, 0, PAD)
    return jnp.concatenate([jnp.where(keep, h[:, s * LANES:(s + 1) * LANES], 0.0)
                            for s in range(width // LANES)], axis=1).astype(BF16)


def _causal_conv(x, halo_ref, cw_ref, cb_ref, cs):
    rows, width = x.shape
    taps = cw_ref.shape[0]
    first_sublane = lax.broadcasted_iota(jnp.int32, (SUBLANES, width), 0) == 0
    tail = halo_ref[:, cs]
    halo_ref[:, cs] = x[rows - (taps - 1) * SUBLANES:, :]
    wrapped = []
    for j in range(1, taps):
        cur = x[rows - j * SUBLANES:rows - (j - 1) * SUBLANES, :]
        prev = tail[(taps - 1 - j) * SUBLANES:(taps - j) * SUBLANES, :]
        wrapped.append(jnp.where(first_sublane, pltpu.roll(prev, 1, 0), pltpu.roll(cur, 1, 0)))
    out = cb_ref[:, cs] + cw_ref[taps - 1:taps, cs] * x
    for j in range(1, taps):
        shifted = jnp.concatenate(wrapped[j - 1::-1] + [x[:rows - j * SUBLANES, :]], axis=0)
        out = out + cw_ref[taps - 1 - j:taps - j, cs] * shifted
    return out


def _input_rows(x_refs, prefix_ref, t):
    first = jnp.where(t > 0, x_refs[0][0], prefix_ref[...])
    return jnp.concatenate([first] + [r[0] for r in x_refs[1:]], axis=0)


def _input_specs():
    n = ROW_TILE // PREFIX
    return [pl.BlockSpec((1, PREFIX, D_MODEL),
                         lambda b, t, k=k: (b, jnp.maximum(n * t - 1 + k, 0), 0))
            for k in range(n)] + [_const_spec((PREFIX, D_MODEL))]


def _mamba_front_kernel(xa_ref, xb_ref, xc_ref, prefix_ref,
                        wz_ref, wxbc_ref, wdt_ref, cw_ref, cb_ref, dtb_ref, alog_ref,
                        z_ref, xbc_ref, dt_ref, da_ref, halo_ref, buf_ref):
    t = pl.program_id(1)
    rows = ROW_TILE
    width = FRONT_CHUNK
    n_x = SSM_CONV_DIM // width
    n_z = SSM_D_INNER // width

    @pl.when(t == 0)
    def _():
        halo_ref[...] = jnp.zeros(halo_ref.shape, F32)

    u = _input_rows((xa_ref, xb_ref, xc_ref), prefix_ref, t).astype(BF16)

    def project(c):
        cs = slice(c * width, (c + 1) * width)
        slot = c % FRONT_BUFFERS
        buf_ref[slot, 0:HALO, :] = halo_ref[:, cs]
        pre = _dot(u, wxbc_ref[:, cs])
        buf_ref[slot, HALO:HALO + rows, :] = pre
        buf_ref[slot, HALO:HALO + PAD, :] = jnp.where(t > 0, pre[:PAD], 0.0)

    def gate(c):
        cs = slice(c * width, (c + 1) * width)
        z_ref[0, :, cs] = _dot(u, wz_ref[:, cs]).astype(BF16)

    for c in range(FRONT_BUFFERS - 1):
        project(c)
    for c in range(n_x):
        if c + FRONT_BUFFERS - 1 < n_x:
            project(c + FRONT_BUFFERS - 1)
        if c < n_z:
            gate(c)
        cs = slice(c * width, (c + 1) * width)
        slot = c % FRONT_BUFFERS
        acc = cb_ref[:, cs]
        for k in range(SSM_CONV):
            off = HALO - (SSM_CONV - 1) + k
            acc = acc + cw_ref[k:k + 1, cs] * buf_ref[slot, off:off + rows, :]
        halo_ref[:, cs] = buf_ref[slot, rows:rows + HALO, :]
        act = _silu(acc).astype(BF16)
        xbc_ref[0, :, cs] = act
        xbc_ref[0, 0:PAD, cs] = jnp.where(t > 0, act[:PAD], jnp.zeros((), BF16))

    x = _dot(u, wdt_ref[...]) + dtb_ref[...]
    dt = jnp.maximum(x, 0.0) + jnp.log1p(jnp.exp(-jnp.abs(x)))
    da = dt * (-jnp.exp(alog_ref[...]) * LOG2_E)
    dt_ref[0] = dt
    da_ref[0] = da
    dt_ref[0, 0:PAD, :] = jnp.where(t > 0, dt[:PAD], 0.0)
    da_ref[0, 0:PAD, :] = jnp.where(t > 0, da[:PAD], 0.0)


def _mamba_front(x, prefix, wz, wxbc, wdt, cw, cb, dtb, alog):
    bsz, seq, _ = x.shape
    lp = PREFIX + seq
    nt = lp // ROW_TILE
    row = lambda w: pl.BlockSpec((1, ROW_TILE, w), lambda b, t: (b, t, 0))
    return pl.pallas_call(
        _mamba_front_kernel,
        grid=(bsz, nt),
        in_specs=_input_specs() + [_const_spec(wz.shape), _const_spec(wxbc.shape),
                  _const_spec(wdt.shape), _const_spec(cw.shape), _const_spec(cb.shape),
                  _const_spec(dtb.shape), _const_spec(alog.shape)],
        out_specs=[row(SSM_D_INNER), row(SSM_CONV_DIM), row(HEAD_LANES), row(HEAD_LANES)],
        out_shape=[jax.ShapeDtypeStruct((bsz, lp, SSM_D_INNER), BF16),
                   jax.ShapeDtypeStruct((bsz, lp, SSM_CONV_DIM), BF16),
                   jax.ShapeDtypeStruct((bsz, lp, HEAD_LANES), F32),
                   jax.ShapeDtypeStruct((bsz, lp, HEAD_LANES), F32)],
        scratch_shapes=[pltpu.VMEM((HALO, SSM_CONV_DIM), F32),
                        pltpu.VMEM((FRONT_BUFFERS, ROW_TILE + HALO, FRONT_CHUNK), F32)],
        compiler_params=pltpu.CompilerParams(
            dimension_semantics=("arbitrary", "arbitrary"), vmem_limit_bytes=VMEM_LIMIT),
        name="mamba_front",
    )(x, x, x, prefix, wz, wxbc, wdt, cw, cb, dtb, alog)


def _mamba_scan_kernel(xa_ref, xb_ref, xc_ref, prefix_ref,
                       z_ref, xs_ref, b_ref, c_ref, dt_ref, da_ref, tri_ref, e_ref, dskip_ref,
                       ng_ref, wout_ref, lng_ref, lnb_ref, out_ref,
                       acs_ref, dte_ref, y_ref, state_ref):
    t = pl.program_id(1)
    rows = ROW_TILE
    n_chunks = rows // CHUNK

    @pl.when(t == 0)
    def _():
        state_ref[...] = jnp.zeros(state_ref.shape, F32)

    acs = sum(_dot(tri_ref[...], p) for p in _split_bf16(da_ref[0], 3))
    acs_ref[...] = _dot(_packed_terms(acs, 3), e_ref[...])
    dte_ref[...] = _dot(_packed_terms(dt_ref[0], 2), e_ref[...])

    s_i = lax.broadcasted_iota(jnp.int32, (CHUNK, 2 * CHUNK), 0)
    l_i = lax.broadcasted_iota(jnp.int32, (CHUNK, 2 * CHUNK), 1)
    diag = (l_i & (CHUNK - 1)) == s_i
    causal = (l_i & (CHUNK - 1)) <= s_i
    br_i = lax.broadcasted_iota(jnp.int32, (2 * CHUNK, 2 * CHUNK), 0)
    bc_i = lax.broadcasted_iota(jnp.int32, (2 * CHUNK, 2 * CHUNK), 1)
    blockdiag = (br_i // CHUNK) == (bc_i // CHUNK)

    def chunk_body(c, carry):
        r0 = pl.multiple_of(c * CHUNK, CHUNK)
        acs_c = acs_ref[pl.ds(r0, CHUNK), :]
        dte_c = dte_ref[pl.ds(r0, CHUNK), :]
        xs_c = xs_ref[0, pl.ds(r0, CHUNK), :]
        b_c = b_ref[0, pl.ds(r0, CHUNK), :]
        c_c = c_ref[0, pl.ds(r0, CHUNK), :]
        last = acs_c[CHUNK - 1:CHUNK, :]
        groups = [slice(g * SSM_STATE, (g + 1) * SSM_STATE) for g in range(SSM_GROUPS)]
        gcols = [slice(g * GROUP_COLS, (g + 1) * GROUP_COLS) for g in range(SSM_GROUPS)]
        pair = 2 * SSM_HEAD_DIM
        states = [state_ref[g] for g in range(SSM_GROUPS)]
        cb2 = [lax.dot_general(c_c[:, gl], jnp.concatenate([b_c[:, gl], b_c[:, gl]], axis=0),
                               (((1,), (1,)), ((), ())), preferred_element_type=F32)
               for gl in groups]
        y_off = [_dot(c_c[:, gl], st.astype(BF16)) for gl, st in zip(groups, states)]
        x_dt = xs_c.astype(F32) * dte_c
        xw = (x_dt * jnp.exp2(last - acs_c)).astype(BF16)
        x_dt = x_dt.astype(BF16)
        s_new = [lax.dot_general(b_c[:, gl], xw[:, gc], (((0,), (0,)), ((), ())),
                                 preferred_element_type=F32) for gl, gc in zip(groups, gcols)]
        grow = jnp.exp2(acs_c)
        chunk_decay = jnp.exp2(last)
        for g in range(SSM_GROUPS):
            for jj in range(GROUP_COLS // pair):
                pl_ = slice(g * GROUP_COLS + jj * pair, g * GROUP_COLS + (jj + 1) * pair)
                a_p = acs_c[:, pl_]
                a_row = jnp.sum(jnp.where(diag, a_p, 0.0), axis=0, keepdims=True)
                decay = jnp.exp2(jnp.where(causal, a_p - a_row, NEG_BIG))
                m2 = (cb2[g] * decay).astype(BF16)
                x_p = x_dt[:, pl_]
                x_bd = jnp.where(blockdiag, jnp.concatenate([x_p, x_p], axis=0),
                                 jnp.zeros((), BF16))
                y_ref[pl.ds(r0, CHUNK), pl_] = (
                    _dot(m2, x_bd) + y_off[g][:, jj * pair:(jj + 1) * pair] * grow[:, pl_])
            state_ref[g] = states[g] * chunk_decay[:, gcols[g]] + s_new[g]
        return carry

    lax.fori_loop(0, n_chunks, chunk_body, 0)

    mix = jnp.zeros((rows, D_MODEL), F32)
    for g in range(SSM_GROUPS):
        gc = slice(g * GROUP_COLS, (g + 1) * GROUP_COLS)
        y = y_ref[:, gc] + xs_ref[0, :, gc].astype(F32) * dskip_ref[:, gc]
        yn = _rms(y * _silu(z_ref[0, :, gc].astype(F32))) * ng_ref[:, gc]
        mix = mix + _dot(yn.astype(BF16), wout_ref[g])
    h = _input_rows((xa_ref, xb_ref, xc_ref), prefix_ref, t)
    out_ref[0] = _layer_norm(DN_ALPHA * h + mix, lng_ref[...], lnb_ref[...])


def _mamba_scan(x, prefix, z, xbc, dt, da, tri, e, dskip, ng, wout, lng, lnb):
    bsz, seq, _ = x.shape
    lp = PREFIX + seq
    nt = lp // ROW_TILE
    row = lambda w, cb=0: pl.BlockSpec((1, ROW_TILE, w), lambda b, t: (b, t, cb))
    return pl.pallas_call(
        _mamba_scan_kernel,
        grid=(bsz, nt),
        in_specs=_input_specs() + [row(SSM_D_INNER), row(SSM_D_INNER),
                  row(SSM_BC, SSM_D_INNER // SSM_BC), row(SSM_BC, SSM_D_INNER // SSM_BC + 1),
                  row(HEAD_LANES), row(HEAD_LANES), _const_spec(tri.shape),
                  _const_spec(e.shape), _const_spec(dskip.shape), _const_spec(ng.shape),
                  _const_spec(wout.shape), _const_spec(lng.shape), _const_spec(lnb.shape)],
        out_specs=row(D_MODEL),
        out_shape=jax.ShapeDtypeStruct((bsz, lp, D_MODEL), F32),
        scratch_shapes=[pltpu.VMEM((ROW_TILE, SSM_D_INNER), F32),
                        pltpu.VMEM((ROW_TILE, SSM_D_INNER), F32),
                        pltpu.VMEM((ROW_TILE, SSM_D_INNER), F32),
                        pltpu.VMEM((SSM_GROUPS, SSM_STATE, GROUP_COLS), F32)],
        compiler_params=pltpu.CompilerParams(
            dimension_semantics=("arbitrary", "arbitrary"), vmem_limit_bytes=VMEM_LIMIT),
        name="mamba_scan",
    )(x, x, x, prefix, z, xbc, xbc, xbc, dt, da, tri, e, dskip, ng, wout, lng, lnb)


def _ffn_body(h, t, win_ref, cw_ref, cb_ref, wout_ref, lng_ref, lnb_ref, out_ref,
              halo_ref, perm_ref):
    rows = ROW_TILE

    @pl.when(t == 0)
    def _():
        halo_ref[...] = jnp.zeros(halo_ref.shape, F32)

    h = _sublane_major(perm_ref, h)
    u = _matmul_rows(h, t)

    def cols(cb):
        return (slice(cb * FF_CHUNK, (cb + 1) * FF_CHUNK),
                slice(D_FF + cb * FF_CHUNK, D_FF + (cb + 1) * FF_CHUNK))

    def conv(x, cs):
        return _causal_conv(x, halo_ref, cw_ref, cb_ref, cs)

    def project(cb):
        return [_dot(u, win_ref[:, cs]) for cs in cols(cb)]

    pre = {cb: project(cb) for cb in range(FF_AHEAD)}
    acc = jnp.zeros((rows, D_MODEL), F32)
    for cb in range(N_FF_CHUNKS):
        if cb + FF_AHEAD < N_FF_CHUNKS:
            pre[cb + FF_AHEAD] = project(cb + FF_AHEAD)
        gate, up = pre.pop(cb)
        gate_cols, up_cols = cols(cb)
        act = (_silu(conv(gate, gate_cols)) * conv(up, up_cols)).astype(BF16)
        acc = acc + _dot(act, wout_ref[cb])
    out = _layer_norm(DN_ALPHA * h + acc, lng_ref[...], lnb_ref[...])
    out_ref[0] = _natural_order(perm_ref, out)


def _ffn_kernel(h_ref, win_ref, cw_ref, cb_ref, wout_ref, lng_ref, lnb_ref, out_ref,
                halo_ref, perm_ref):
    _ffn_body(h_ref[0], pl.program_id(1), win_ref, cw_ref, cb_ref, wout_ref, lng_ref, lnb_ref,
              out_ref, halo_ref, perm_ref)


def _attn_out_ffn_kernel(h_ref, o_ref, wo_ref, lng0_ref, lnb0_ref, win_ref, cw_ref, cb_ref,
                         wout_ref, lng_ref, lnb_ref, out_ref, halo_ref, perm_ref):
    mix = _dot(o_ref[0], wo_ref[...])
    h = _layer_norm(DN_ALPHA * h_ref[0] + mix, lng0_ref[...], lnb0_ref[...])
    _ffn_body(h, pl.program_id(1), win_ref, cw_ref, cb_ref, wout_ref, lng_ref, lnb_ref,
              out_ref, halo_ref, perm_ref)


def _ffn(h, win, cw, cb, wout, lng, lnb, attn=None):
    bsz, lp, _ = h.shape
    nt = lp // ROW_TILE
    row = pl.BlockSpec((1, ROW_TILE, D_MODEL), lambda b, t: (b, t, 0))
    consts = (win, cw, cb, wout, lng, lnb)
    if attn is None:
        kern, args, specs = _ffn_kernel, (h,) + consts, [row]
    else:
        o, wo, lng0, lnb0 = attn
        kern = _attn_out_ffn_kernel
        args = (h, o, wo, lng0, lnb0) + consts
        specs = [row, row, _const_spec(wo.shape), _const_spec(lng0.shape),
                 _const_spec(lnb0.shape)]
    return pl.pallas_call(
        kern,
        grid=(bsz, nt),
        in_specs=specs + [_const_spec(a.shape) for a in consts],
        out_specs=row,
        out_shape=jax.ShapeDtypeStruct((bsz, lp, D_MODEL), F32),
        scratch_shapes=[pltpu.VMEM((2 * SUBLANES, 2 * D_FF), F32),
                        pltpu.VMEM((D_MODEL // LANES, ROW_TILE, LANES), F32)],
        compiler_params=pltpu.CompilerParams(
            dimension_semantics=("arbitrary", "arbitrary"), vmem_limit_bytes=VMEM_LIMIT),
        name="conv_ffn" if attn is None else "attn_out_conv_ffn",
    )(*args)


def _mla_proj_kernel(h_ref, wkvd_ref, kvg_ref, wk_ref, wv_ref, wpe_ref, cosk_ref,
                     sink_ref, wqd_ref, qg_ref, wq_ref, cosq_ref, sinq_ref,
                     q_ref, k_ref, v_ref):
    half = QK_ROPE // 2
    lane = lax.broadcasted_iota(jnp.int32, (1, HEAD_PAD), 1)
    first_half = lane < QK_NOPE + half

    def rope(x, cos, sin_signed):
        swapped = jnp.where(first_half, pltpu.roll(x, HEAD_PAD - half, 1),
                            pltpu.roll(x, half, 1))
        return x * cos + swapped * sin_signed

    u = h_ref[0].astype(BF16)
    cq = (_rms(_dot(u, wqd_ref[...])) * qg_ref[...]).astype(BF16)
    q_all = _dot(cq, wq_ref[...])
    k_pe = rope(_dot(u, wpe_ref[...]), cosk_ref[...], sink_ref[...])
    ckv = (_rms(_dot(u, wkvd_ref[...])) * kvg_ref[...]).astype(BF16)
    k_all = _dot(ckv, wk_ref[...])
    v_all = _dot(ckv, wv_ref[...])
    ones_col = jnp.where(lane == V_HEAD, 1.0, 0.0)
    cosq = cosq_ref[...]
    sinq = sinq_ref[...]
    for hd in range(MLA_HEADS):
        hl = slice(hd * HEAD_PAD, (hd + 1) * HEAD_PAD)
        q_ref[0, hd] = rope(q_all[:, hl], cosq, sinq).astype(BF16)
    for hd in range(MLA_HEADS):
        hl = slice(hd * HEAD_PAD, (hd + 1) * HEAD_PAD)
        k_ref[0, hd] = (k_all[:, hl] + k_pe).astype(BF16)
        v_ref[0, hd] = (v_all[:, hl] + ones_col).astype(BF16)


def _mla_proj(h, wkvd, kvg, wk, wv, wpe, cosk, sink, wqd, qg, wq, cosq, sinq):
    bsz, lp, _ = h.shape
    nt = lp // ROW_TILE
    row = pl.BlockSpec((1, ROW_TILE, D_MODEL), lambda b, t: (b, t, 0))
    tab = pl.BlockSpec((ROW_TILE, HEAD_PAD), lambda b, t: (t, 0))
    head = pl.BlockSpec((1, MLA_HEADS, ROW_TILE, HEAD_PAD), lambda b, t: (b, 0, t, 0))
    shape = jax.ShapeDtypeStruct((bsz, MLA_HEADS, lp, HEAD_PAD), BF16)
    c = _const_spec
    return pl.pallas_call(
        _mla_proj_kernel,
        grid=(bsz, nt),
        in_specs=[row, c(wkvd.shape), c(kvg.shape), c(wk.shape), c(wv.shape), c(wpe.shape),
                  tab, tab, c(wqd.shape), c(qg.shape), c(wq.shape), tab, tab],
        out_specs=[head, head, head],
        out_shape=[shape, shape, shape],
        compiler_params=pltpu.CompilerParams(
            dimension_semantics=("arbitrary", "arbitrary"), vmem_limit_bytes=VMEM_LIMIT),
        name="mla_proj",
    )(h, wkvd, kvg, wk, wv, wpe, cosk, sink, wqd, qg, wq, cosq, sinq)


def _attention_kernel(q_ref, k_ref, v_ref, o_ref, m_ref, acc_ref):
    i = pl.program_id(2)
    rows = ROW_TILE
    blk = LANES
    n_blk = rows // blk
    shift = CHUNK.bit_length() - 1
    col = lax.broadcasted_iota(jnp.int32, (rows, blk), 1)
    b_row = lax.broadcasted_iota(jnp.int32, (blk, blk), 0)
    b_col = lax.broadcasted_iota(jnp.int32, (blk, blk), 1)
    causal_blk = (b_col >> shift) <= (b_row >> shift)
    qs = [q_ref[0, hh] for hh in range(ATT_HEADS)]

    def real_keys(j, diag):
        return (b_col if diag else col) + j * rows >= PAD

    def scores(j, hh, valid0, diag):
        k0 = pl.multiple_of(j * rows, rows)
        s = lax.dot_general(qs[hh], k_ref[0, hh, pl.ds(k0, rows), :],
                            (((1,), (1,)), ((), ())), preferred_element_type=F32)
        if not diag:
            return jnp.concatenate([jnp.where(valid0, s[:, :blk], NEG_BIG), s[:, blk:]], axis=1)
        out = []
        for rb in range(n_blk):
            r_s = slice(rb * blk, (rb + 1) * blk)
            row = []
            for cb in range(n_blk):
                if cb > rb:
                    row.append(jnp.full((blk, blk), NEG_BIG, F32))
                    continue
                sb = s[r_s, cb * blk:(cb + 1) * blk]
                ok = causal_blk if cb == rb else None
                if cb == 0:
                    ok = valid0 if ok is None else ok & valid0
                row.append(sb if ok is None else jnp.where(ok, sb, NEG_BIG))
            out.append(jnp.concatenate(row, axis=1))
        return jnp.concatenate(out, axis=0)

    def absorb(j, hh, s):
        k0 = pl.multiple_of(j * rows, rows)
        m = m_ref[hh]
        m_new = jnp.maximum(m, jnp.max(s, axis=-1, keepdims=True))
        p = jnp.exp2(s - jnp.concatenate([m_new] * n_blk, axis=1)).astype(BF16)
        m_ref[hh] = m_new
        acc_ref[hh] = (jnp.exp2(m - m_new) * acc_ref[hh]
                       + _dot(p, v_ref[0, hh, pl.ds(k0, rows), :]))

    def steps(tiles, diag=False):
        valid = [real_keys(j, diag) for j in tiles]
        cur = [scores(tiles[0], hh, valid[0], diag) for hh in range(ATT_HEADS)]
        for n, j in enumerate(tiles):
            nxt = []
            for hh in range(ATT_HEADS):
                if n + 1 < len(tiles):
                    nxt.append(scores(tiles[n + 1], hh, valid[n + 1], diag))
                absorb(j, hh, cur[hh])
            cur = nxt

    m_ref[...] = jnp.full(m_ref.shape, NEG_BIG, F32)
    acc_ref[...] = jnp.zeros(acc_ref.shape, F32)
    start = 0
    for width, trips in ((1, i & 1), (2, (i >> 1) & 1), (4, i >> 2)):
        def group(g, _, start=start, width=width):
            steps([start + width * g + w for w in range(width)])
            return 0
        lax.fori_loop(0, trips, group, 0)
        start = start + width * trips
    steps([i], diag=True)
    outs = []
    for hh in range(ATT_HEADS):
        acc = acc_ref[hh]
        outs.append(acc[:, :V_HEAD] / acc[:, V_HEAD:V_HEAD + 1])
    o_ref[0] = jnp.concatenate(outs, axis=1).astype(BF16)


def _attention(q, k, v):
    bsz, nh, lp, _ = q.shape
    nt = lp // ROW_TILE
    hs = ATT_HEADS
    return pl.pallas_call(
        _attention_kernel,
        grid=(bsz, nh // hs, nt),
        in_specs=[pl.BlockSpec((1, hs, ROW_TILE, HEAD_PAD), lambda b, hp, i: (b, hp, i, 0)),
                  pl.BlockSpec((1, hs, lp, HEAD_PAD), lambda b, hp, i: (b, hp, 0, 0)),
                  pl.BlockSpec((1, hs, lp, HEAD_PAD), lambda b, hp, i: (b, hp, 0, 0))],
        out_specs=pl.BlockSpec((1, ROW_TILE, hs * V_HEAD), lambda b, hp, i: (b, i, hp)),
        out_shape=jax.ShapeDtypeStruct((bsz, lp, MLA_HEADS * V_HEAD), BF16),
        scratch_shapes=[pltpu.VMEM((hs, ROW_TILE, LANES), F32),
                        pltpu.VMEM((hs, ROW_TILE, HEAD_PAD), F32)],
        compiler_params=pltpu.CompilerParams(
            dimension_semantics=("arbitrary", "arbitrary", "arbitrary"),
            vmem_limit_bytes=VMEM_LIMIT),
        name="mla_attention",
    )(q, k, v)


def _row(v, width=None):
    v = v.astype(F32).reshape(1, -1)
    if width is not None and v.shape[1] < width:
        v = jnp.pad(v, ((0, 0), (0, width - v.shape[1])))
    return v


def _ffn_params(w_in, conv_w, conv_b, w_out):
    return (w_in.astype(BF16), conv_w.astype(F32), _row(conv_b),
            w_out.astype(BF16).reshape(N_FF_CHUNKS, FF_CHUNK, D_MODEL))


def _rope_tables(lp):
    half = QK_ROPE // 2
    pos = jnp.maximum(jnp.arange(lp, dtype=F32) - PAD, 0.0)
    inv_freq = ROPE_THETA ** (-jnp.arange(half, dtype=F32) * 2.0 / QK_ROPE)
    ang = pos[:, None] * inv_freq[None, :]
    cos, sin = jnp.cos(ang), jnp.sin(ang)
    zeros = lambda n: jnp.zeros((lp, n), F32)
    tail = HEAD_PAD - QK_NOPE - QK_ROPE
    cosk = jnp.concatenate([zeros(QK_NOPE), cos, cos, zeros(tail)], axis=1)
    sink = jnp.concatenate([zeros(QK_NOPE), -sin, sin, zeros(tail)], axis=1)
    c = (QK_NOPE + QK_ROPE) ** -0.5 * LOG2_E
    cosq = c * jnp.concatenate([jnp.ones((lp, QK_NOPE), F32), cos, cos, zeros(tail)], axis=1)
    return cosk, sink, cosq, c * sink


def _pad_last(a, before, total):
    pads = [(0, 0)] * (a.ndim - 1) + [(before, total - before - a.shape[-1])]
    return jnp.pad(a, pads)


def kernel(x, meta_tokens, m_w_in, m_conv_w, m_conv_b, m_dt_bias, m_a_log, m_d, m_norm_g, m_w_out, kv_w_down, kv_norm_g, kv_w_up, q_w_down, q_norm_g, q_w_up, o_w, f_w_in, f_conv_w, f_conv_b, f_w_out, ln_g, ln_b):
    bsz, seq, _ = x.shape
    lp = PAD + N_META + seq
    assert lp % ROW_TILE == 0 and PREFIX % CHUNK == 0 and ROW_TILE % PREFIX == 0
    assert seq % PREFIX == 0
    prefix = jnp.concatenate([jnp.zeros((PAD, D_MODEL), x.dtype), meta_tokens.astype(x.dtype)],
                             axis=0)

    w_in = m_w_in[0]
    wz = w_in[:, :SSM_D_INNER].astype(BF16)
    wxbc = w_in[:, SSM_D_INNER:SSM_D_INNER + SSM_CONV_DIM].astype(BF16)
    copies = HEAD_LANES // SSM_HEADS
    wdt = jnp.tile(w_in[:, SSM_D_INNER + SSM_CONV_DIM:], (1, copies)).astype(BF16)
    head_of_col = jnp.arange(SSM_D_INNER) // SSM_HEAD_DIM
    expand = (jnp.arange(HEAD_LANES)[:, None] % SSM_HEADS == head_of_col[None, :]).astype(BF16)
    pos = jnp.arange(ROW_TILE)
    tri = ((pos[:, None] // CHUNK == pos[None, :] // CHUNK)
           & (pos[None, :] <= pos[:, None])).astype(BF16)
    z, xbc, dt, da = _mamba_front(x, prefix, wz, wxbc, wdt, m_conv_w[0].astype(F32),
                                  _row(m_conv_b[0]), _row(jnp.tile(m_dt_bias[0], copies)),
                                  _row(jnp.tile(m_a_log[0], copies)))
    h = _mamba_scan(x, prefix, z, xbc, dt, da, tri, expand,
                    _row(jnp.repeat(m_d[0], SSM_HEAD_DIM)),
                    _row(m_norm_g[0]),
                    m_w_out[0].astype(BF16).reshape(SSM_GROUPS, GROUP_COLS, D_MODEL),
                    _row(ln_g[0, 0]), _row(ln_b[0, 0]))
    h = _ffn(h, *_ffn_params(f_w_in[0], f_conv_w[0], f_conv_b[0], f_w_out[0]),
             _row(ln_g[0, 1]), _row(ln_b[0, 1]))

    cosk, sink, cosq, sinq = _rope_tables(lp)
    w_rope = kv_w_down[:, KV_LORA:]
    wpe = _pad_last(w_rope, QK_NOPE, HEAD_PAD).astype(BF16)
    kv_up = kv_w_up.reshape(KV_LORA, MLA_HEADS, QK_NOPE + V_HEAD)
    wk = _pad_last(kv_up[..., :QK_NOPE], 0, HEAD_PAD).reshape(KV_LORA, -1).astype(BF16)
    wv = _pad_last(kv_up[..., QK_NOPE:], 0, HEAD_PAD).reshape(KV_LORA, -1).astype(BF16)
    q_up = q_w_up[0].reshape(Q_LORA, MLA_HEADS, QK_NOPE + QK_ROPE)
    wq = _pad_last(q_up, 0, HEAD_PAD).reshape(Q_LORA, -1).astype(BF16)
    q, k, v = _mla_proj(h, kv_w_down[:, :KV_LORA].astype(BF16), _row(kv_norm_g), wk, wv, wpe,
                        cosk, sink, q_w_down[0].astype(BF16), _row(q_norm_g[0]), wq, cosq, sinq)
    o = _attention(q, k, v)
    h = _ffn(h, *_ffn_params(f_w_in[1], f_conv_w[1], f_conv_b[1], f_w_out[1]),
             _row(ln_g[1, 1]), _row(ln_b[1, 1]),
             attn=(o, o_w[0].astype(BF16), _row(ln_g[1, 0]), _row(ln_b[1, 0])))
    return h[:, PAD + N_META:]
```

```python
import math

import jax
import jax.numpy as jnp
from jax import lax
from jax.experimental import pallas as pl
from jax.experimental.pallas import tpu as pltpu

F32 = jnp.float32
BF16 = jnp.bfloat16

D_MODEL = 1024
N_META = 16
CHUNK = 64
DEPTH = 2
DN_ALPHA = (2.0 * DEPTH) ** 0.25
LN_EPS = 1e-5
RMS_EPS = 1e-6
LOG2_E = math.log2(math.e)

SSM_D_INNER = 2 * D_MODEL
SSM_HEAD_DIM = 64
SSM_HEADS = SSM_D_INNER // SSM_HEAD_DIM
SSM_GROUPS = 4
SSM_STATE = 128
SSM_CONV = 4
SSM_BC = SSM_GROUPS * SSM_STATE
SSM_CONV_DIM = SSM_D_INNER + 2 * SSM_BC
GROUP_COLS = SSM_D_INNER // SSM_GROUPS
HEAD_LANES = 128
FRONT_CHUNK = 512
FRONT_BUFFERS = 3

MLA_HEADS = 16
Q_LORA = 384
KV_LORA = 256
QK_NOPE = 64
QK_ROPE = 32
V_HEAD = 64
ROPE_THETA = 10000.0
HEAD_PAD = 128
ATT_HEADS = 8

D_FF = 2816
FFN_CONV = 3
FF_CHUNK = 256
N_FF_CHUNKS = D_FF // FF_CHUNK
FF_AHEAD = 3
SUBLANES = 8
LANES = 128

PAD = 112
PREFIX = PAD + N_META
ROW_TILE = 384
HALO = 8
NEG_BIG = -1e30
VMEM_LIMIT = 60 * 1024 * 1024


def _const_spec(shape):
    nd = len(shape)
    return pl.BlockSpec(shape, lambda *_: (0,) * nd, pipeline_mode=pl.Buffered(1))


def _layer_norm(v, g, b):
    mu = jnp.mean(v, axis=-1, keepdims=True)
    d = v - mu
    var = jnp.mean(d * d, axis=-1, keepdims=True)
    return d * lax.rsqrt(var + LN_EPS) * g + b


def _rms(v):
    return v * lax.rsqrt(jnp.mean(v * v, axis=-1, keepdims=True) + RMS_EPS)


def _silu(v):
    return v / (1.0 + jnp.exp(-v))


def _split_bf16(v, parts):
    out = []
    rem = v
    for _ in range(parts):
        p = rem.astype(BF16)
        out.append(p)
        rem = rem - p.astype(F32)
    return out


def _dot(a, b):
    return jnp.dot(a, b, preferred_element_type=F32)


def _packed_terms(v, parts):
    lane_copy = lax.broadcasted_iota(jnp.int32, v.shape, 1) // SSM_HEADS
    out = jnp.zeros(v.shape, BF16)
    for k, term in enumerate(_split_bf16(v, parts)):
        out = jnp.where(lane_copy == k, term, out)
    return out


def _sublane_major(perm_ref, v):
    rows, width = v.shape
    stride = rows // SUBLANES
    n = width // LANES
    for s in range(n):
        perm_ref[s] = v[:, s * LANES:(s + 1) * LANES]
    return jnp.concatenate(
        [jnp.concatenate([perm_ref[s, pl.ds(a, SUBLANES, stride=stride), :]
                          for s in range(n)], axis=1) for a in range(stride)], axis=0)


def _natural_order(perm_ref, v):
    rows, width = v.shape
    stride = rows // SUBLANES
    n = width // LANES
    for a in range(stride):
        for s in range(n):
            perm_ref[s, pl.ds(a, SUBLANES, stride=stride), :] = (
                v[a * SUBLANES:(a + 1) * SUBLANES, s * LANES:(s + 1) * LANES])
    return jnp.concatenate([perm_ref[s] for s in range(n)], axis=1)


def _matmul_rows(h, t):
    rows, width = h.shape
    p_i = lax.broadcasted_iota(jnp.int32, (rows, LANES), 0)
    time = (p_i % SUBLANES) * (rows // SUBLANES) + p_i // SUBLANES
    keep = time >= jnp.where(t > 0, 0, PAD)
    return jnp.concatenate([jnp.where(keep, h[:, s * LANES:(s + 1) * LANES], 0.0)
                            for s in range(width // LANES)], axis=1).astype(BF16)


def _causal_conv(x, halo_ref, cw_ref, cb_ref, cs):
    rows, width = x.shape
    taps = cw_ref.shape[0]
    first_sublane = lax.broadcasted_iota(jnp.int32, (SUBLANES, width), 0) == 0
    tail = halo_ref[:, cs]
    halo_ref[:, cs] = x[rows - (taps - 1) * SUBLANES:, :]
    wrapped = []
    for j in range(1, taps):
        cur = x[rows - j * SUBLANES:rows - (j - 1) * SUBLANES, :]
        prev = tail[(taps - 1 - j) * SUBLANES:(taps - j) * SUBLANES, :]
        wrapped.append(jnp.where(first_sublane, pltpu.roll(prev, 1, 0), pltpu.roll(cur, 1, 0)))
    out = cb_ref[:, cs] + cw_ref[taps - 1:taps, cs] * x
    for j in range(1, taps):
        shifted = jnp.concatenate(wrapped[j - 1::-1] + [x[:rows - j * SUBLANES, :]], axis=0)
        out = out + cw_ref[taps - 1 - j:taps - j, cs] * shifted
    return out


def _input_rows(x_refs, prefix_ref, t):
    first = jnp.where(t > 0, x_refs[0][0], prefix_ref[...])
    return jnp.concatenate([first] + [r[0] for r in x_refs[1:]], axis=0)


def _input_specs():
    n = ROW_TILE // PREFIX
    return [pl.BlockSpec((1, PREFIX, D_MODEL),
                         lambda b, t, k=k: (b, jnp.maximum(n * t - 1 + k, 0), 0))
            for k in range(n)] + [_const_spec((PREFIX, D_MODEL))]


def _mamba_front_kernel(xa_ref, xb_ref, xc_ref, prefix_ref,
                        wz_ref, wxbc_ref, wdt_ref, cw_ref, cb_ref, dtb_ref, alog_ref,
                        z_ref, xbc_ref, dt_ref, da_ref, halo_ref, buf_ref):
    t = pl.program_id(1)
    rows = ROW_TILE
    width = FRONT_CHUNK
    n_x = SSM_CONV_DIM // width
    n_z = SSM_D_INNER // width

    @pl.when(t == 0)
    def _():
        halo_ref[...] = jnp.zeros(halo_ref.shape, F32)

    u = _input_rows((xa_ref, xb_ref, xc_ref), prefix_ref, t).astype(BF16)

    def project(c):
        cs = slice(c * width, (c + 1) * width)
        slot = c % FRONT_BUFFERS
        buf_ref[slot, 0:HALO, :] = halo_ref[:, cs]
        pre = _dot(u, wxbc_ref[:, cs])
        buf_ref[slot, HALO:HALO + rows, :] = pre
        buf_ref[slot, HALO:HALO + PAD, :] = jnp.where(t > 0, pre[:PAD], 0.0)

    def gate(c):
        cs = slice(c * width, (c + 1) * width)
        z_ref[0, :, cs] = _dot(u, wz_ref[:, cs]).astype(BF16)

    for c in range(FRONT_BUFFERS - 1):
        project(c)
    for c in range(n_x):
        if c + FRONT_BUFFERS - 1 < n_x:
            project(c + FRONT_BUFFERS - 1)
        if c < n_z:
            gate(c)
        cs = slice(c * width, (c + 1) * width)
        slot = c % FRONT_BUFFERS
        acc = cb_ref[:, cs]
        for k in range(SSM_CONV):
            off = HALO - (SSM_CONV - 1) + k
            acc = acc + cw_ref[k:k + 1, cs] * buf_ref[slot, off:off + rows, :]
        halo_ref[:, cs] = buf_ref[slot, rows:rows + HALO, :]
        act = _silu(acc).astype(BF16)
        xbc_ref[0, :, cs] = act
        xbc_ref[0, 0:PAD, cs] = jnp.where(t > 0, act[:PAD], jnp.zeros((), BF16))

    x = _dot(u, wdt_ref[...]) + dtb_ref[...]
    dt = jnp.maximum(x, 0.0) + jnp.log1p(jnp.exp(-jnp.abs(x)))
    da = dt * (-jnp.exp(alog_ref[...]) * LOG2_E)
    dt_ref[0] = dt
    da_ref[0] = da
    dt_ref[0, 0:PAD, :] = jnp.where(t > 0, dt[:PAD], 0.0)
    da_ref[0, 0:PAD, :] = jnp.where(t > 0, da[:PAD], 0.0)


def _mamba_front(x, prefix, wz, wxbc, wdt, cw, cb, dtb, alog):
    bsz, seq, _ = x.shape
    lp = PREFIX + seq
    nt = lp // ROW_TILE
    row = lambda w: pl.BlockSpec((1, ROW_TILE, w), lambda b, t: (b, t, 0))
    return pl.pallas_call(
        _mamba_front_kernel,
        grid=(bsz, nt),
        in_specs=_input_specs() + [_const_spec(wz.shape), _const_spec(wxbc.shape),
                  _const_spec(wdt.shape), _const_spec(cw.shape), _const_spec(cb.shape),
                  _const_spec(dtb.shape), _const_spec(alog.shape)],
        out_specs=[row(SSM_D_INNER), row(SSM_CONV_DIM), row(HEAD_LANES), row(HEAD_LANES)],
        out_shape=[jax.ShapeDtypeStruct((bsz, lp, SSM_D_INNER), BF16),
                   jax.ShapeDtypeStruct((bsz, lp, SSM_CONV_DIM), BF16),
                   jax.ShapeDtypeStruct((bsz, lp, HEAD_LANES), F32),
                   jax.ShapeDtypeStruct((bsz, lp, HEAD_LANES), F32)],
        scratch_shapes=[pltpu.VMEM((HALO, SSM_CONV_DIM), F32),
                        pltpu.VMEM((FRONT_BUFFERS, ROW_TILE + HALO, FRONT_CHUNK), F32)],
        compiler_params=pltpu.CompilerParams(
            dimension_semantics=("arbitrary", "arbitrary"), vmem_limit_bytes=VMEM_LIMIT),
        name="mamba_front",
    )(x, x, x, prefix, wz, wxbc, wdt, cw, cb, dtb, alog)


def _mamba_scan_kernel(xa_ref, xb_ref, xc_ref, prefix_ref,
                       z_ref, xs_ref, b_ref, c_ref, dt_ref, da_ref, tri_ref, e_ref, dskip_ref,
                       ng_ref, wout_ref, lng_ref, lnb_ref, out_ref,
                       acs_ref, dte_ref, y_ref, state_ref):
    t = pl.program_id(1)
    rows = ROW_TILE
    n_chunks = rows // CHUNK

    @pl.when(t == 0)
    def _():
        state_ref[...] = jnp.zeros(state_ref.shape, F32)

    acs = sum(_dot(tri_ref[...], p) for p in _split_bf16(da_ref[0], 3))
    acs_ref[...] = _dot(_packed_terms(acs, 3), e_ref[...])
    dte_ref[...] = _dot(_packed_terms(dt_ref[0], 2), e_ref[...])

    s_i = lax.broadcasted_iota(jnp.int32, (CHUNK, 2 * CHUNK), 0)
    l_i = lax.broadcasted_iota(jnp.int32, (CHUNK, 2 * CHUNK), 1)
    diag = (l_i & (CHUNK - 1)) == s_i
    causal = (l_i & (CHUNK - 1)) <= s_i
    br_i = lax.broadcasted_iota(jnp.int32, (2 * CHUNK, 2 * CHUNK), 0)
    bc_i = lax.broadcasted_iota(jnp.int32, (2 * CHUNK, 2 * CHUNK), 1)
    blockdiag = (br_i // CHUNK) == (bc_i // CHUNK)

    def chunk_body(c, carry):
        r0 = pl.multiple_of(c * CHUNK, CHUNK)
        acs_c = acs_ref[pl.ds(r0, CHUNK), :]
        dte_c = dte_ref[pl.ds(r0, CHUNK), :]
        xs_c = xs_ref[0, pl.ds(r0, CHUNK), :]
        b_c = b_ref[0, pl.ds(r0, CHUNK), :]
        c_c = c_ref[0, pl.ds(r0, CHUNK), :]
        last = acs_c[CHUNK - 1:CHUNK, :]
        groups = [slice(g * SSM_STATE, (g + 1) * SSM_STATE) for g in range(SSM_GROUPS)]
        gcols = [slice(g * GROUP_COLS, (g + 1) * GROUP_COLS) for g in range(SSM_GROUPS)]
        pair = 2 * SSM_HEAD_DIM
        states = [state_ref[g] for g in range(SSM_GROUPS)]
        cb2 = [lax.dot_general(c_c[:, gl], jnp.concatenate([b_c[:, gl], b_c[:, gl]], axis=0),
                               (((1,), (1,)), ((), ())), preferred_element_type=F32)
               for gl in groups]
        y_off = [_dot(c_c[:, gl], st.astype(BF16)) for gl, st in zip(groups, states)]
        x_dt = xs_c.astype(F32) * dte_c
        xw = (x_dt * jnp.exp2(last - acs_c)).astype(BF16)
        x_dt = x_dt.astype(BF16)
        s_new = [lax.dot_general(b_c[:, gl], xw[:, gc], (((0,), (0,)), ((), ())),
                                 preferred_element_type=F32) for gl, gc in zip(groups, gcols)]
        grow = jnp.exp2(acs_c)
        chunk_decay = jnp.exp2(last)
        for g in range(SSM_GROUPS):
            for jj in range(GROUP_COLS // pair):
                pl_ = slice(g * GROUP_COLS + jj * pair, g * GROUP_COLS + (jj + 1) * pair)
                a_p = acs_c[:, pl_]
                a_row = jnp.sum(jnp.where(diag, a_p, 0.0), axis=0, keepdims=True)
                decay = jnp.exp2(jnp.where(causal, a_p - a_row, NEG_BIG))
                m2 = (cb2[g] * decay).astype(BF16)
                x_p = x_dt[:, pl_]
                x_bd = jnp.where(blockdiag, jnp.concatenate([x_p, x_p], axis=0),
                                 jnp.zeros((), BF16))
                y_ref[pl.ds(r0, CHUNK), pl_] = (
                    _dot(m2, x_bd) + y_off[g][:, jj * pair:(jj + 1) * pair] * grow[:, pl_])
            state_ref[g] = states[g] * chunk_decay[:, gcols[g]] + s_new[g]
        return carry

    lax.fori_loop(0, n_chunks, chunk_body, 0)

    mix = jnp.zeros((rows, D_MODEL), F32)
    for g in range(SSM_GROUPS):
        gc = slice(g * GROUP_COLS, (g + 1) * GROUP_COLS)
        y = y_ref[:, gc] + xs_ref[0, :, gc].astype(F32) * dskip_ref[:, gc]
        yn = _rms(y * _silu(z_ref[0, :, gc].astype(F32))) * ng_ref[:, gc]
        mix = mix + _dot(yn.astype(BF16), wout_ref[g])
    h = _input_rows((xa_ref, xb_ref, xc_ref), prefix_ref, t)
    out_ref[0] = _layer_norm(DN_ALPHA * h + mix, lng_ref[...], lnb_ref[...])


def _mamba_scan(x, prefix, z, xbc, dt, da, tri, e, dskip, ng, wout, lng, lnb):
    bsz, seq, _ = x.shape
    lp = PREFIX + seq
    nt = lp // ROW_TILE
    row = lambda w, cb=0: pl.BlockSpec((1, ROW_TILE, w), lambda b, t: (b, t, cb))
    return pl.pallas_call(
        _mamba_scan_kernel,
        grid=(bsz, nt),
        in_specs=_input_specs() + [row(SSM_D_INNER), row(SSM_D_INNER),
                  row(SSM_BC, SSM_D_INNER // SSM_BC), row(SSM_BC, SSM_D_INNER // SSM_BC + 1),
                  row(HEAD_LANES), row(HEAD_LANES), _const_spec(tri.shape),
                  _const_spec(e.shape), _const_spec(dskip.shape), _const_spec(ng.shape),
                  _const_spec(wout.shape), _const_spec(lng.shape), _const_spec(lnb.shape)],
        out_specs=row(D_MODEL),
        out_shape=jax.ShapeDtypeStruct((bsz, lp, D_MODEL), F32),
        scratch_shapes=[pltpu.VMEM((ROW_TILE, SSM_D_INNER), F32),
                        pltpu.VMEM((ROW_TILE, SSM_D_INNER), F32),
                        pltpu.VMEM((ROW_TILE, SSM_D_INNER), F32),
                        pltpu.VMEM((SSM_GROUPS, SSM_STATE, GROUP_COLS), F32)],
        compiler_params=pltpu.CompilerParams(
            dimension_semantics=("arbitrary", "arbitrary"), vmem_limit_bytes=VMEM_LIMIT),
        name="mamba_scan",
    )(x, x, x, prefix, z, xbc, xbc, xbc, dt, da, tri, e, dskip, ng, wout, lng, lnb)


def _ffn_steps(load_h, win_ref, cw_ref, cb_ref, wout_ref, lng_ref, lnb_ref, out_ref,
               halo_ref, perm_ref, pend_ref, unperm_ref):
    t = pl.program_id(1)
    last = pl.num_programs(1) - 1

    def finish():
        out = _layer_norm(pend_ref[...], lng_ref[...], lnb_ref[...])
        out_ref[0] = _natural_order(unperm_ref, out)

    @pl.when(t == 0)
    def _():
        halo_ref[...] = jnp.zeros(halo_ref.shape, F32)
        pend_ref[...] = jnp.zeros(pend_ref.shape, F32)

    @pl.when(t < last)
    def _():
        finish()
        pend_ref[...] = _ffn_tile(load_h(), t, win_ref, cw_ref, cb_ref, wout_ref, halo_ref,
                                  perm_ref)

    @pl.when(t == last)
    def _():
        finish()


def _ffn_tile(h, t, win_ref, cw_ref, cb_ref, wout_ref, halo_ref, perm_ref):
    rows = ROW_TILE
    h = _sublane_major(perm_ref, h)
    u = _matmul_rows(h, t)

    def cols(cb):
        return (slice(cb * FF_CHUNK, (cb + 1) * FF_CHUNK),
                slice(D_FF + cb * FF_CHUNK, D_FF + (cb + 1) * FF_CHUNK))

    def conv(x, cs):
        return _causal_conv(x, halo_ref, cw_ref, cb_ref, cs)

    def project(cb):
        return [_dot(u, win_ref[:, cs]) for cs in cols(cb)]

    pre = {cb: project(cb) for cb in range(FF_AHEAD)}
    acc = jnp.zeros((rows, D_MODEL), F32)
    for cb in range(N_FF_CHUNKS):
        if cb + FF_AHEAD < N_FF_CHUNKS:
            pre[cb + FF_AHEAD] = project(cb + FF_AHEAD)
        gate, up = pre.pop(cb)
        gate_cols, up_cols = cols(cb)
        act = (_silu(conv(gate, gate_cols)) * conv(up, up_cols)).astype(BF16)
        acc = acc + _dot(act, wout_ref[cb])
    return DN_ALPHA * h + acc


def _ffn_kernel(h_ref, *refs):
    _ffn_steps(lambda: h_ref[0], *refs)


def _attn_out_ffn_kernel(h_ref, o_ref, wo_ref, lng0_ref, lnb0_ref, *refs):
    def load_h():
        mix = _dot(o_ref[0], wo_ref[...])
        return _layer_norm(DN_ALPHA * h_ref[0] + mix, lng0_ref[...], lnb0_ref[...])
    _ffn_steps(load_h, *refs)


def _ffn(h, win, cw, cb, wout, lng, lnb, attn=None):
    bsz, lp, _ = h.shape
    nt = lp // ROW_TILE
    row_in = pl.BlockSpec((1, ROW_TILE, D_MODEL), lambda b, t: (b, jnp.minimum(t, nt - 1), 0))
    row_out = pl.BlockSpec((1, ROW_TILE, D_MODEL), lambda b, t: (b, jnp.maximum(t - 1, 0), 0))
    consts = (win, cw, cb, wout, lng, lnb)
    if attn is None:
        kern, args, specs = _ffn_kernel, (h,) + consts, [row_in]
    else:
        o, wo, lng0, lnb0 = attn
        kern = _attn_out_ffn_kernel
        args = (h, o, wo, lng0, lnb0) + consts
        specs = [row_in, row_in, _const_spec(wo.shape), _const_spec(lng0.shape),
                 _const_spec(lnb0.shape)]
    slabs = pltpu.VMEM((D_MODEL // LANES, ROW_TILE, LANES), F32)
    return pl.pallas_call(
        kern,
        grid=(bsz, nt + 1),
        in_specs=specs + [_const_spec(a.shape) for a in consts],
        out_specs=row_out,
        out_shape=jax.ShapeDtypeStruct((bsz, lp, D_MODEL), F32),
        scratch_shapes=[pltpu.VMEM((2 * SUBLANES, 2 * D_FF), F32), slabs,
                        pltpu.VMEM((ROW_TILE, D_MODEL), F32), slabs],
        compiler_params=pltpu.CompilerParams(
            dimension_semantics=("arbitrary", "arbitrary"), vmem_limit_bytes=VMEM_LIMIT),
        name="conv_ffn" if attn is None else "attn_out_conv_ffn",
    )(*args)


def _mla_proj_kernel(h_ref, wkvd_ref, kvg_ref, wk_ref, wv_ref, wpe_ref, cosk_ref,
                     sink_ref, wqd_ref, qg_ref, wq_ref, cosq_ref, sinq_ref,
                     q_ref, k_ref, v_ref):
    half = QK_ROPE // 2
    lane = lax.broadcasted_iota(jnp.int32, (1, HEAD_PAD), 1)
    first_half = lane < QK_NOPE + half

    def rope(x, cos, sin_signed):
        swapped = jnp.where(first_half, pltpu.roll(x, HEAD_PAD - half, 1),
                            pltpu.roll(x, half, 1))
        return x * cos + swapped * sin_signed

    u = h_ref[0].astype(BF16)
    cq = (_rms(_dot(u, wqd_ref[...])) * qg_ref[...]).astype(BF16)
    q_all = _dot(cq, wq_ref[...])
    k_pe = rope(_dot(u, wpe_ref[...]), cosk_ref[...], sink_ref[...])
    ckv = (_rms(_dot(u, wkvd_ref[...])) * kvg_ref[...]).astype(BF16)
    k_all = _dot(ckv, wk_ref[...])
    v_all = _dot(ckv, wv_ref[...])
    ones_col = jnp.where(lane == V_HEAD, 1.0, 0.0)
    cosq = cosq_ref[...]
    sinq = sinq_ref[...]
    for hd in range(MLA_HEADS):
        hl = slice(hd * HEAD_PAD, (hd + 1) * HEAD_PAD)
        q_ref[0, hd] = rope(q_all[:, hl], cosq, sinq).astype(BF16)
    for hd in range(MLA_HEADS):
        hl = slice(hd * HEAD_PAD, (hd + 1) * HEAD_PAD)
        k_ref[0, hd] = (k_all[:, hl] + k_pe).astype(BF16)
        v_ref[0, hd] = (v_all[:, hl] + ones_col).astype(BF16)


def _mla_proj(h, wkvd, kvg, wk, wv, wpe, cosk, sink, wqd, qg, wq, cosq, sinq):
    bsz, lp, _ = h.shape
    nt = lp // ROW_TILE
    row = pl.BlockSpec((1, ROW_TILE, D_MODEL), lambda b, t: (b, t, 0))
    tab = pl.BlockSpec((ROW_TILE, HEAD_PAD), lambda b, t: (t, 0))
    head = pl.BlockSpec((1, MLA_HEADS, ROW_TILE, HEAD_PAD), lambda b, t: (b, 0, t, 0))
    shape = jax.ShapeDtypeStruct((bsz, MLA_HEADS, lp, HEAD_PAD), BF16)
    c = _const_spec
    return pl.pallas_call(
        _mla_proj_kernel,
        grid=(bsz, nt),
        in_specs=[row, c(wkvd.shape), c(kvg.shape), c(wk.shape), c(wv.shape), c(wpe.shape),
                  tab, tab, c(wqd.shape), c(qg.shape), c(wq.shape), tab, tab],
        out_specs=[head, head, head],
        out_shape=[shape, shape, shape],
        compiler_params=pltpu.CompilerParams(
            dimension_semantics=("arbitrary", "arbitrary"), vmem_limit_bytes=VMEM_LIMIT),
        name="mla_proj",
    )(h, wkvd, kvg, wk, wv, wpe, cosk, sink, wqd, qg, wq, cosq, sinq)


def _attention_kernel(q_ref, k_ref, v_ref, o_ref, m_ref, acc_ref):
    i = pl.program_id(2)
    rows = ROW_TILE
    blk = LANES
    n_blk = rows // blk
    shift = CHUNK.bit_length() - 1
    col = lax.broadcasted_iota(jnp.int32, (rows, blk), 1)
    b_row = lax.broadcasted_iota(jnp.int32, (blk, blk), 0)
    b_col = lax.broadcasted_iota(jnp.int32, (blk, blk), 1)
    causal_blk = (b_col >> shift) <= (b_row >> shift)
    qs = [q_ref[0, hh] for hh in range(ATT_HEADS)]

    def real_keys(j, diag):
        return (b_col if diag else col) + j * rows >= PAD

    def scores(j, hh, valid0, diag):
        k0 = pl.multiple_of(j * rows, rows)
        s = lax.dot_general(qs[hh], k_ref[0, hh, pl.ds(k0, rows), :],
                            (((1,), (1,)), ((), ())), preferred_element_type=F32)
        if not diag:
            return jnp.concatenate([jnp.where(valid0, s[:, :blk], NEG_BIG), s[:, blk:]], axis=1)
        out = []
        for rb in range(n_blk):
            r_s = slice(rb * blk, (rb + 1) * blk)
            row = []
            for cb in range(n_blk):
                if cb > rb:
                    row.append(jnp.full((blk, blk), NEG_BIG, F32))
                    continue
                sb = s[r_s, cb * blk:(cb + 1) * blk]
                ok = causal_blk if cb == rb else None
                if cb == 0:
                    ok = valid0 if ok is None else ok & valid0
                row.append(sb if ok is None else jnp.where(ok, sb, NEG_BIG))
            out.append(jnp.concatenate(row, axis=1))
        return jnp.concatenate(out, axis=0)

    def absorb(j, hh, s):
        k0 = pl.multiple_of(j * rows, rows)
        m = m_ref[hh]
        m_new = jnp.maximum(m, jnp.max(s, axis=-1, keepdims=True))
        p = jnp.exp2(s - jnp.concatenate([m_new] * n_blk, axis=1)).astype(BF16)
        m_ref[hh] = m_new
        acc_ref[hh] = (jnp.exp2(m - m_new) * acc_ref[hh]
                       + _dot(p, v_ref[0, hh, pl.ds(k0, rows), :]))

    def steps(tiles, diag=False):
        valid = [real_keys(j, diag) for j in tiles]
        cur = [scores(tiles[0], hh, valid[0], diag) for hh in range(ATT_HEADS)]
        for n, j in enumerate(tiles):
            nxt = []
            for hh in range(ATT_HEADS):
                if n + 1 < len(tiles):
                    nxt.append(scores(tiles[n + 1], hh, valid[n + 1], diag))
                absorb(j, hh, cur[hh])
            cur = nxt

    m_ref[...] = jnp.full(m_ref.shape, NEG_BIG, F32)
    acc_ref[...] = jnp.zeros(acc_ref.shape, F32)
    start = 0
    for width, trips in ((1, i & 1), (2, (i >> 1) & 1), (4, i >> 2)):
        def group(g, _, start=start, width=width):
            steps([start + width * g + w for w in range(width)])
            return 0
        lax.fori_loop(0, trips, group, 0)
        start = start + width * trips
    steps([i], diag=True)
    outs = []
    for hh in range(ATT_HEADS):
        acc = acc_ref[hh]
        outs.append(acc[:, :V_HEAD] / acc[:, V_HEAD:V_HEAD + 1])
    o_ref[0] = jnp.concatenate(outs, axis=1).astype(BF16)


def _attention(q, k, v):
    bsz, nh, lp, _ = q.shape
    nt = lp // ROW_TILE
    hs = ATT_HEADS
    return pl.pallas_call(
        _attention_kernel,
        grid=(bsz, nh // hs, nt),
        in_specs=[pl.BlockSpec((1, hs, ROW_TILE, HEAD_PAD), lambda b, hp, i: (b, hp, i, 0)),
                  pl.BlockSpec((1, hs, lp, HEAD_PAD), lambda b, hp, i: (b, hp, 0, 0)),
                  pl.BlockSpec((1, hs, lp, HEAD_PAD), lambda b, hp, i: (b, hp, 0, 0))],
        out_specs=pl.BlockSpec((1, ROW_TILE, hs * V_HEAD), lambda b, hp, i: (b, i, hp)),
        out_shape=jax.ShapeDtypeStruct((bsz, lp, MLA_HEADS * V_HEAD), BF16),
        scratch_shapes=[pltpu.VMEM((hs, ROW_TILE, LANES), F32),
                        pltpu.VMEM((hs, ROW_TILE, HEAD_PAD), F32)],
        compiler_params=pltpu.CompilerParams(
            dimension_semantics=("arbitrary", "arbitrary", "arbitrary"),
            vmem_limit_bytes=VMEM_LIMIT),
        name="mla_attention",
    )(q, k, v)


def _row(v, width=None):
    v = v.astype(F32).reshape(1, -1)
    if width is not None and v.shape[1] < width:
        v = jnp.pad(v, ((0, 0), (0, width - v.shape[1])))
    return v


def _ffn_params(w_in, conv_w, conv_b, w_out):
    return (w_in.astype(BF16), conv_w.astype(F32), _row(conv_b),
            w_out.astype(BF16).reshape(N_FF_CHUNKS, FF_CHUNK, D_MODEL))


def _rope_tables(lp):
    half = QK_ROPE // 2
    pos = jnp.maximum(jnp.arange(lp, dtype=F32) - PAD, 0.0)
    inv_freq = ROPE_THETA ** (-jnp.arange(half, dtype=F32) * 2.0 / QK_ROPE)
    ang = pos[:, None] * inv_freq[None, :]
    cos, sin = jnp.cos(ang), jnp.sin(ang)
    zeros = lambda n: jnp.zeros((lp, n), F32)
    tail = HEAD_PAD - QK_NOPE - QK_ROPE
    cosk = jnp.concatenate([zeros(QK_NOPE), cos, cos, zeros(tail)], axis=1)
    sink = jnp.concatenate([zeros(QK_NOPE), -sin, sin, zeros(tail)], axis=1)
    c = (QK_NOPE + QK_ROPE) ** -0.5 * LOG2_E
    cosq = c * jnp.concatenate([jnp.ones((lp, QK_NOPE), F32), cos, cos, zeros(tail)], axis=1)
    return cosk, sink, cosq, c * sink


def _pad_last(a, before, total):
    pads = [(0, 0)] * (a.ndim - 1) + [(before, total - before - a.shape[-1])]
    return jnp.pad(a, pads)


def kernel(x, meta_tokens, m_w_in, m_conv_w, m_conv_b, m_dt_bias, m_a_log, m_d, m_norm_g, m_w_out, kv_w_down, kv_norm_g, kv_w_up, q_w_down, q_norm_g, q_w_up, o_w, f_w_in, f_conv_w, f_conv_b, f_w_out, ln_g, ln_b):
    bsz, seq, _ = x.shape
    lp = PAD + N_META + seq
    assert lp % ROW_TILE == 0 and PREFIX % CHUNK == 0 and ROW_TILE % PREFIX == 0
    assert seq % PREFIX == 0
    prefix = jnp.concatenate([jnp.zeros((PAD, D_MODEL), x.dtype), meta_tokens.astype(x.dtype)],
                             axis=0)

    w_in = m_w_in[0]
    wz = w_in[:, :SSM_D_INNER].astype(BF16)
    wxbc = w_in[:, SSM_D_INNER:SSM_D_INNER + SSM_CONV_DIM].astype(BF16)
    copies = HEAD_LANES // SSM_HEADS
    wdt = jnp.tile(w_in[:, SSM_D_INNER + SSM_CONV_DIM:], (1, copies)).astype(BF16)
    head_of_col = jnp.arange(SSM_D_INNER) // SSM_HEAD_DIM
    expand = (jnp.arange(HEAD_LANES)[:, None] % SSM_HEADS == head_of_col[None, :]).astype(BF16)
    pos = jnp.arange(ROW_TILE)
    tri = ((pos[:, None] // CHUNK == pos[None, :] // CHUNK)
           & (pos[None, :] <= pos[:, None])).astype(BF16)
    z, xbc, dt, da = _mamba_front(x, prefix, wz, wxbc, wdt, m_conv_w[0].astype(F32),
                                  _row(m_conv_b[0]), _row(jnp.tile(m_dt_bias[0], copies)),
                                  _row(jnp.tile(m_a_log[0], copies)))
    h = _mamba_scan(x, prefix, z, xbc, dt, da, tri, expand,
                    _row(jnp.repeat(m_d[0], SSM_HEAD_DIM)),
                    _row(m_norm_g[0]),
                    m_w_out[0].astype(BF16).reshape(SSM_GROUPS, GROUP_COLS, D_MODEL),
                    _row(ln_g[0, 0]), _row(ln_b[0, 0]))
    h = _ffn(h, *_ffn_params(f_w_in[0], f_conv_w[0], f_conv_b[0], f_w_out[0]),
             _row(ln_g[0, 1]), _row(ln_b[0, 1]))

    cosk, sink, cosq, sinq = _rope_tables(lp)
    w_rope = kv_w_down[:, KV_LORA:]
    wpe = _pad_last(w_rope, QK_NOPE, HEAD_PAD).astype(BF16)
    kv_up = kv_w_up.reshape(KV_LORA, MLA_HEADS, QK_NOPE + V_HEAD)
    wk = _pad_last(kv_up[..., :QK_NOPE], 0, HEAD_PAD).reshape(KV_LORA, -1).astype(BF16)
    wv = _pad_last(kv_up[..., QK_NOPE:], 0, HEAD_PAD).reshape(KV_LORA, -1).astype(BF16)
    q_up = q_w_up[0].reshape(Q_LORA, MLA_HEADS, QK_NOPE + QK_ROPE)
    wq = _pad_last(q_up, 0, HEAD_PAD).reshape(Q_LORA, -1).astype(BF16)
    q, k, v = _mla_proj(h, kv_w_down[:, :KV_LORA].astype(BF16), _row(kv_norm_g), wk, wv, wpe,
                        cosk, sink, q_w_down[0].astype(BF16), _row(q_norm_g[0]), wq, cosq, sinq)
    o = _attention(q, k, v)
    h = _ffn(h, *_ffn_params(f_w_in[1], f_conv_w[1], f_conv_b[1], f_w_out[1]),
             _row(ln_g[1, 1]), _row(ln_b[1, 1]),
             attn=(o, o_w[0].astype(BF16), _row(ln_g[1, 0]), _row(ln_b[1, 0])))
    return h[:, PAD + N_META:]
```

```python
import math

import jax
import jax.numpy as jnp
from jax import lax
from jax.experimental import pallas as pl
from jax.experimental.pallas import tpu as pltpu

F32 = jnp.float32
BF16 = jnp.bfloat16

D_MODEL = 1024
N_META = 16
CHUNK = 64
DEPTH = 2
DN_ALPHA = (2.0 * DEPTH) ** 0.25
LN_EPS = 1e-5
RMS_EPS = 1e-6
LOG2_E = math.log2(math.e)

SSM_D_INNER = 2 * D_MODEL
SSM_HEAD_DIM = 64
SSM_HEADS = SSM_D_INNER // SSM_HEAD_DIM
SSM_GROUPS = 4
SSM_STATE = 128
SSM_CONV = 4
SSM_BC = SSM_GROUPS * SSM_STATE
SSM_CONV_DIM = SSM_D_INNER + 2 * SSM_BC
GROUP_COLS = SSM_D_INNER // SSM_GROUPS
HEAD_LANES = 128
FRONT_CHUNK = 512
FRONT_BUFFERS = 3

MLA_HEADS = 16
Q_LORA = 384
KV_LORA = 256
QK_NOPE = 64
QK_ROPE = 32
V_HEAD = 64
ROPE_THETA = 10000.0
HEAD_PAD = 128
ATT_HEADS = 8

D_FF = 2816
FFN_CONV = 3
FF_CHUNK = 256
N_FF_CHUNKS = D_FF // FF_CHUNK
FF_AHEAD = 3
SUBLANES = 8
LANES = 128

PAD = 112
PREFIX = PAD + N_META
ROW_TILE = 384
HALO = 8
NEG_BIG = -1e30
VMEM_LIMIT = 60 * 1024 * 1024


def _const_spec(shape):
    nd = len(shape)
    return pl.BlockSpec(shape, lambda *_: (0,) * nd, pipeline_mode=pl.Buffered(1))


def _layer_norm(v, g, b):
    mu = jnp.mean(v, axis=-1, keepdims=True)
    d = v - mu
    var = jnp.mean(d * d, axis=-1, keepdims=True)
    return d * lax.rsqrt(var + LN_EPS) * g + b


def _rms(v):
    return v * lax.rsqrt(jnp.mean(v * v, axis=-1, keepdims=True) + RMS_EPS)


def _silu(v):
    return v / (1.0 + jnp.exp(-v))


def _split_bf16(v, parts):
    out = []
    rem = v
    for _ in range(parts):
        p = rem.astype(BF16)
        out.append(p)
        rem = rem - p.astype(F32)
    return out


def _dot(a, b):
    return jnp.dot(a, b, preferred_element_type=F32)


def _packed_terms(v, parts):
    lane_copy = lax.broadcasted_iota(jnp.int32, v.shape, 1) // SSM_HEADS
    out = jnp.zeros(v.shape, BF16)
    for k, term in enumerate(_split_bf16(v, parts)):
        out = jnp.where(lane_copy == k, term, out)
    return out


def _sublane_major(perm_ref, v):
    rows, width = v.shape
    stride = rows // SUBLANES
    n = width // LANES
    for s in range(n):
        perm_ref[s] = v[:, s * LANES:(s + 1) * LANES]
    return jnp.concatenate(
        [jnp.concatenate([perm_ref[s, pl.ds(a, SUBLANES, stride=stride), :]
                          for s in range(n)], axis=1) for a in range(stride)], axis=0)


def _natural_order(perm_ref, v):
    rows, width = v.shape
    stride = rows // SUBLANES
    n = width // LANES
    for a in range(stride):
        for s in range(n):
            perm_ref[s, pl.ds(a, SUBLANES, stride=stride), :] = (
                v[a * SUBLANES:(a + 1) * SUBLANES, s * LANES:(s + 1) * LANES])
    return jnp.concatenate([perm_ref[s] for s in range(n)], axis=1)


def _matmul_rows(h, t):
    rows, width = h.shape
    p_i = lax.broadcasted_iota(jnp.int32, (rows, LANES), 0)
    time = (p_i % SUBLANES) * (rows // SUBLANES) + p_i // SUBLANES
    keep = time >= jnp.where(t > 0, 0, PAD)
    return jnp.concatenate([jnp.where(keep, h[:, s * LANES:(s + 1) * LANES], 0.0)
                            for s in range(width // LANES)], axis=1).astype(BF16)


def _causal_conv(x, halo_ref, cw_ref, cb_ref, cs):
    rows, width = x.shape
    taps = cw_ref.shape[0]
    first_sublane = lax.broadcasted_iota(jnp.int32, (SUBLANES, width), 0) == 0
    tail = halo_ref[:, cs]
    halo_ref[:, cs] = x[rows - (taps - 1) * SUBLANES:, :]
    wrapped = []
    for j in range(1, taps):
        cur = x[rows - j * SUBLANES:rows - (j - 1) * SUBLANES, :]
        prev = tail[(taps - 1 - j) * SUBLANES:(taps - j) * SUBLANES, :]
        wrapped.append(jnp.where(first_sublane, pltpu.roll(prev, 1, 0), pltpu.roll(cur, 1, 0)))
    out = cb_ref[:, cs] + cw_ref[taps - 1:taps, cs] * x
    for j in range(1, taps):
        shifted = jnp.concatenate(wrapped[j - 1::-1] + [x[:rows - j * SUBLANES, :]], axis=0)
        out = out + cw_ref[taps - 1 - j:taps - j, cs] * shifted
    return out


def _input_rows(x_refs, prefix_ref, t):
    first = jnp.where(t > 0, x_refs[0][0], prefix_ref[...])
    return jnp.concatenate([first] + [r[0] for r in x_refs[1:]], axis=0)


def _input_specs():
    n = ROW_TILE // PREFIX
    return [pl.BlockSpec((1, PREFIX, D_MODEL),
                         lambda b, t, k=k: (b, jnp.maximum(n * t - 1 + k, 0), 0))
            for k in range(n)] + [_const_spec((PREFIX, D_MODEL))]


def _mamba_front_kernel(xa_ref, xb_ref, xc_ref, prefix_ref,
                        wz_ref, wxbc_ref, wdt_ref, cw_ref, cb_ref, dtb_ref, alog_ref,
                        z_ref, xbc_ref, dt_ref, da_ref, halo_ref, buf_ref):
    t = pl.program_id(1)
    rows = ROW_TILE
    width = FRONT_CHUNK
    n_x = SSM_CONV_DIM // width
    n_z = SSM_D_INNER // width

    @pl.when(t == 0)
    def _():
        halo_ref[...] = jnp.zeros(halo_ref.shape, F32)

    u = _input_rows((xa_ref, xb_ref, xc_ref), prefix_ref, t).astype(BF16)

    def project(c):
        cs = slice(c * width, (c + 1) * width)
        slot = c % FRONT_BUFFERS
        buf_ref[slot, 0:HALO, :] = halo_ref[:, cs]
        pre = _dot(u, wxbc_ref[:, cs])
        buf_ref[slot, HALO:HALO + rows, :] = pre
        buf_ref[slot, HALO:HALO + PAD, :] = jnp.where(t > 0, pre[:PAD], 0.0)

    def gate(c):
        cs = slice(c * width, (c + 1) * width)
        z_ref[0, :, cs] = _dot(u, wz_ref[:, cs]).astype(BF16)

    for c in range(FRONT_BUFFERS - 1):
        project(c)
    for c in range(n_x):
        if c + FRONT_BUFFERS - 1 < n_x:
            project(c + FRONT_BUFFERS - 1)
        if c < n_z:
            gate(c)
        cs = slice(c * width, (c + 1) * width)
        slot = c % FRONT_BUFFERS
        acc = cb_ref[:, cs]
        for k in range(SSM_CONV):
            off = HALO - (SSM_CONV - 1) + k
            acc = acc + cw_ref[k:k + 1, cs] * buf_ref[slot, off:off + rows, :]
        halo_ref[:, cs] = buf_ref[slot, rows:rows + HALO, :]
        act = _silu(acc).astype(BF16)
        xbc_ref[0, :, cs] = act
        xbc_ref[0, 0:PAD, cs] = jnp.where(t > 0, act[:PAD], jnp.zeros((), BF16))

    x = _dot(u, wdt_ref[...]) + dtb_ref[...]
    dt = jnp.maximum(x, 0.0) + jnp.log1p(jnp.exp(-jnp.abs(x)))
    da = dt * (-jnp.exp(alog_ref[...]) * LOG2_E)
    dt_ref[0] = dt
    da_ref[0] = da
    dt_ref[0, 0:PAD, :] = jnp.where(t > 0, dt[:PAD], 0.0)
    da_ref[0, 0:PAD, :] = jnp.where(t > 0, da[:PAD], 0.0)


def _mamba_front(x, prefix, wz, wxbc, wdt, cw, cb, dtb, alog):
    bsz, seq, _ = x.shape
    lp = PREFIX + seq
    nt = lp // ROW_TILE
    row = lambda w: pl.BlockSpec((1, ROW_TILE, w), lambda b, t: (b, t, 0))
    return pl.pallas_call(
        _mamba_front_kernel,
        grid=(bsz, nt),
        in_specs=_input_specs() + [_const_spec(wz.shape), _const_spec(wxbc.shape),
                  _const_spec(wdt.shape), _const_spec(cw.shape), _const_spec(cb.shape),
                  _const_spec(dtb.shape), _const_spec(alog.shape)],
        out_specs=[row(SSM_D_INNER), row(SSM_CONV_DIM), row(HEAD_LANES), row(HEAD_LANES)],
        out_shape=[jax.ShapeDtypeStruct((bsz, lp, SSM_D_INNER), BF16),
                   jax.ShapeDtypeStruct((bsz, lp, SSM_CONV_DIM), BF16),
                   jax.ShapeDtypeStruct((bsz, lp, HEAD_LANES), F32),
                   jax.ShapeDtypeStruct((bsz, lp, HEAD_LANES), F32)],
        scratch_shapes=[pltpu.VMEM((HALO, SSM_CONV_DIM), F32),
                        pltpu.VMEM((FRONT_BUFFERS, ROW_TILE + HALO, FRONT_CHUNK), F32)],
        compiler_params=pltpu.CompilerParams(
            dimension_semantics=("arbitrary", "arbitrary"), vmem_limit_bytes=VMEM_LIMIT),
        name="mamba_front",
    )(x, x, x, prefix, wz, wxbc, wdt, cw, cb, dtb, alog)


def _mamba_scan_kernel(xa_ref, xb_ref, xc_ref, prefix_ref,
                       z_ref, xs_ref, b_ref, c_ref, dt_ref, da_ref, tri_ref, e_ref, dskip_ref,
                       ng_ref, wout_ref, lng_ref, lnb_ref, out_ref,
                       acs_ref, dte_ref, y_ref, state_ref):
    t = pl.program_id(1)
    rows = ROW_TILE
    n_chunks = rows // CHUNK

    @pl.when(t == 0)
    def _():
        state_ref[...] = jnp.zeros(state_ref.shape, F32)

    acs = sum(_dot(tri_ref[...], p) for p in _split_bf16(da_ref[0], 3))
    acs_ref[...] = _dot(_packed_terms(acs, 3), e_ref[...])
    dte_ref[...] = _dot(_packed_terms(dt_ref[0], 2), e_ref[...])

    s_i = lax.broadcasted_iota(jnp.int32, (CHUNK, 2 * CHUNK), 0)
    l_i = lax.broadcasted_iota(jnp.int32, (CHUNK, 2 * CHUNK), 1)
    diag = (l_i & (CHUNK - 1)) == s_i
    causal = (l_i & (CHUNK - 1)) <= s_i
    br_i = lax.broadcasted_iota(jnp.int32, (2 * CHUNK, 2 * CHUNK), 0)
    bc_i = lax.broadcasted_iota(jnp.int32, (2 * CHUNK, 2 * CHUNK), 1)
    blockdiag = (br_i // CHUNK) == (bc_i // CHUNK)

    def chunk_body(c, carry):
        r0 = pl.multiple_of(c * CHUNK, CHUNK)
        acs_c = acs_ref[pl.ds(r0, CHUNK), :]
        dte_c = dte_ref[pl.ds(r0, CHUNK), :]
        xs_c = xs_ref[0, pl.ds(r0, CHUNK), :]
        b_c = b_ref[0, pl.ds(r0, CHUNK), :]
        c_c = c_ref[0, pl.ds(r0, CHUNK), :]
        last = acs_c[CHUNK - 1:CHUNK, :]
        groups = [slice(g * SSM_STATE, (g + 1) * SSM_STATE) for g in range(SSM_GROUPS)]
        gcols = [slice(g * GROUP_COLS, (g + 1) * GROUP_COLS) for g in range(SSM_GROUPS)]
        pair = 2 * SSM_HEAD_DIM
        states = [state_ref[g] for g in range(SSM_GROUPS)]
        cb2 = [lax.dot_general(c_c[:, gl], jnp.concatenate([b_c[:, gl], b_c[:, gl]], axis=0),
                               (((1,), (1,)), ((), ())), preferred_element_type=F32)
               for gl in groups]
        y_off = [_dot(c_c[:, gl], st.astype(BF16)) for gl, st in zip(groups, states)]
        x_dt = xs_c.astype(F32) * dte_c
        xw = (x_dt * jnp.exp2(last - acs_c)).astype(BF16)
        x_dt = x_dt.astype(BF16)
        s_new = [lax.dot_general(b_c[:, gl], xw[:, gc], (((0,), (0,)), ((), ())),
                                 preferred_element_type=F32) for gl, gc in zip(groups, gcols)]
        grow = jnp.exp2(acs_c)
        chunk_decay = jnp.exp2(last)
        for g in range(SSM_GROUPS):
            for jj in range(GROUP_COLS // pair):
                pl_ = slice(g * GROUP_COLS + jj * pair, g * GROUP_COLS + (jj + 1) * pair)
                a_p = acs_c[:, pl_]
                a_row = jnp.sum(jnp.where(diag, a_p, 0.0), axis=0, keepdims=True)
                decay = jnp.exp2(jnp.where(causal, a_p - a_row, NEG_BIG))
                m2 = (cb2[g] * decay).astype(BF16)
                x_p = x_dt[:, pl_]
                x_bd = jnp.where(blockdiag, jnp.concatenate([x_p, x_p], axis=0),
                                 jnp.zeros((), BF16))
                y_ref[pl.ds(r0, CHUNK), pl_] = (
                    _dot(m2, x_bd) + y_off[g][:, jj * pair:(jj + 1) * pair] * grow[:, pl_])
            state_ref[g] = states[g] * chunk_decay[:, gcols[g]] + s_new[g]
        return carry

    lax.fori_loop(0, n_chunks, chunk_body, 0)

    mix = jnp.zeros((rows, D_MODEL), F32)
    for g in range(SSM_GROUPS):
        gc = slice(g * GROUP_COLS, (g + 1) * GROUP_COLS)
        y = y_ref[:, gc] + xs_ref[0, :, gc].astype(F32) * dskip_ref[:, gc]
        yn = _rms(y * _silu(z_ref[0, :, gc].astype(F32))) * ng_ref[:, gc]
        mix = mix + _dot(yn.astype(BF16), wout_ref[g])
    h = _input_rows((xa_ref, xb_ref, xc_ref), prefix_ref, t)
    out_ref[0] = _layer_norm(DN_ALPHA * h + mix, lng_ref[...], lnb_ref[...])


def _mamba_scan(x, prefix, z, xbc, dt, da, tri, e, dskip, ng, wout, lng, lnb):
    bsz, seq, _ = x.shape
    lp = PREFIX + seq
    nt = lp // ROW_TILE
    row = lambda w, cb=0: pl.BlockSpec((1, ROW_TILE, w), lambda b, t: (b, t, cb))
    return pl.pallas_call(
        _mamba_scan_kernel,
        grid=(bsz, nt),
        in_specs=_input_specs() + [row(SSM_D_INNER), row(SSM_D_INNER),
                  row(SSM_BC, SSM_D_INNER // SSM_BC), row(SSM_BC, SSM_D_INNER // SSM_BC + 1),
                  row(HEAD_LANES), row(HEAD_LANES), _const_spec(tri.shape),
                  _const_spec(e.shape), _const_spec(dskip.shape), _const_spec(ng.shape),
                  _const_spec(wout.shape), _const_spec(lng.shape), _const_spec(lnb.shape)],
        out_specs=row(D_MODEL),
        out_shape=jax.ShapeDtypeStruct((bsz, lp, D_MODEL), F32),
        scratch_shapes=[pltpu.VMEM((ROW_TILE, SSM_D_INNER), F32),
                        pltpu.VMEM((ROW_TILE, SSM_D_INNER), F32),
                        pltpu.VMEM((ROW_TILE, SSM_D_INNER), F32),
                        pltpu.VMEM((SSM_GROUPS, SSM_STATE, GROUP_COLS), F32)],
        compiler_params=pltpu.CompilerParams(
            dimension_semantics=("arbitrary", "arbitrary"), vmem_limit_bytes=VMEM_LIMIT),
        name="mamba_scan",
    )(x, x, x, prefix, z, xbc, xbc, xbc, dt, da, tri, e, dskip, ng, wout, lng, lnb)


def _ffn_body(h, t, win_ref, cw_ref, cb_ref, wout_ref, lng_ref, lnb_ref, halo_ref, perm_ref):
    rows = ROW_TILE

    @pl.when(t == 0)
    def _():
        halo_ref[...] = jnp.zeros(halo_ref.shape, F32)

    h = _sublane_major(perm_ref, h)
    u = _matmul_rows(h, t)

    def cols(cb):
        return (slice(cb * FF_CHUNK, (cb + 1) * FF_CHUNK),
                slice(D_FF + cb * FF_CHUNK, D_FF + (cb + 1) * FF_CHUNK))

    def conv(x, cs):
        return _causal_conv(x, halo_ref, cw_ref, cb_ref, cs)

    def project(cb):
        return [_dot(u, win_ref[:, cs]) for cs in cols(cb)]

    pre = {cb: project(cb) for cb in range(FF_AHEAD)}
    acc = jnp.zeros((rows, D_MODEL), F32)
    for cb in range(N_FF_CHUNKS):
        if cb + FF_AHEAD < N_FF_CHUNKS:
            pre[cb + FF_AHEAD] = project(cb + FF_AHEAD)
        gate, up = pre.pop(cb)
        gate_cols, up_cols = cols(cb)
        act = (_silu(conv(gate, gate_cols)) * conv(up, up_cols)).astype(BF16)
        acc = acc + _dot(act, wout_ref[cb])
    out = _layer_norm(DN_ALPHA * h + acc, lng_ref[...], lnb_ref[...])
    return _natural_order(perm_ref, out)


def _ffn_kernel(h_ref, win_ref, cw_ref, cb_ref, wout_ref, lng_ref, lnb_ref, out_ref,
                halo_ref, perm_ref):
    out_ref[0] = _ffn_body(h_ref[0], pl.program_id(1), win_ref, cw_ref, cb_ref, wout_ref,
                           lng_ref, lnb_ref, halo_ref, perm_ref)


def _attn_out_ffn_kernel(h_ref, o_ref, wo_ref, lng0_ref, lnb0_ref, win_ref, cw_ref, cb_ref,
                         wout_ref, lng_ref, lnb_ref, out_hbm, halo_ref, perm_ref, stage_ref,
                         sem):
    b, t = pl.program_id(0), pl.program_id(1)
    nt = pl.num_programs(1)
    rows = ROW_TILE
    mix = _dot(o_ref[0], wo_ref[...])
    h = _layer_norm(DN_ALPHA * h_ref[0] + mix, lng0_ref[...], lnb0_ref[...])
    step = b * nt + t
    slot = step % 2
    stage_ref[slot] = _ffn_body(h, t, win_ref, cw_ref, cb_ref, wout_ref, lng_ref, lnb_ref,
                                halo_ref, perm_ref)

    def first_tile_copy(slot_, b_):
        return pltpu.make_async_copy(stage_ref.at[slot_, PREFIX:rows],
                                     out_hbm.at[b_, 0:rows - PREFIX], sem.at[slot_])

    def tile_copy(slot_, b_, t_):
        start = pl.multiple_of(t_ * rows - PREFIX, PREFIX)
        return pltpu.make_async_copy(stage_ref.at[slot_],
                                     out_hbm.at[b_, pl.ds(start, rows)], sem.at[slot_])

    @pl.when(t == 1)
    def _():
        first_tile_copy(1 - slot, b).wait()

    @pl.when(t > 1)
    def _():
        tile_copy(1 - slot, b, t - 1).wait()

    @pl.when((t == 0) & (b > 0))
    def _():
        tile_copy(1 - slot, b - 1, nt - 1).wait()

    @pl.when(t == 0)
    def _():
        first_tile_copy(slot, b).start()

    @pl.when(t > 0)
    def _():
        tile_copy(slot, b, t).start()

    @pl.when(step == pl.num_programs(0) * nt - 1)
    def _():
        tile_copy(slot, b, t).wait()


def _ffn(h, win, cw, cb, wout, lng, lnb, attn=None):
    bsz, lp, _ = h.shape
    nt = lp // ROW_TILE
    row = pl.BlockSpec((1, ROW_TILE, D_MODEL), lambda b, t: (b, t, 0))
    consts = (win, cw, cb, wout, lng, lnb)
    scratch = [pltpu.VMEM((2 * SUBLANES, 2 * D_FF), F32),
               pltpu.VMEM((D_MODEL // LANES, ROW_TILE, LANES), F32)]
    if attn is None:
        kern, args, specs = _ffn_kernel, (h,) + consts, [row]
        out_spec = row
        out_shape = jax.ShapeDtypeStruct((bsz, lp, D_MODEL), F32)
    else:
        o, wo, lng0, lnb0 = attn
        kern = _attn_out_ffn_kernel
        args = (h, o, wo, lng0, lnb0) + consts
        specs = [row, row, _const_spec(wo.shape), _const_spec(lng0.shape),
                 _const_spec(lnb0.shape)]
        out_spec = pl.BlockSpec(memory_space=pl.ANY)
        out_shape = jax.ShapeDtypeStruct((bsz, lp - PREFIX, D_MODEL), F32)
        scratch += [pltpu.VMEM((2, ROW_TILE, D_MODEL), F32), pltpu.SemaphoreType.DMA((2,))]
    return pl.pallas_call(
        kern,
        grid=(bsz, nt),
        in_specs=specs + [_const_spec(a.shape) for a in consts],
        out_specs=out_spec,
        out_shape=out_shape,
        scratch_shapes=scratch,
        compiler_params=pltpu.CompilerParams(
            dimension_semantics=("arbitrary", "arbitrary"), vmem_limit_bytes=VMEM_LIMIT),
        name="conv_ffn" if attn is None else "attn_out_conv_ffn",
    )(*args)


def _mla_proj_kernel(h_ref, wkvd_ref, kvg_ref, wk_ref, wv_ref, wpe_ref, cosk_ref,
                     sink_ref, wqd_ref, qg_ref, wq_ref, cosq_ref, sinq_ref,
                     q_ref, k_ref, v_ref):
    half = QK_ROPE // 2
    lane = lax.broadcasted_iota(jnp.int32, (1, HEAD_PAD), 1)
    first_half = lane < QK_NOPE + half

    def rope(x, cos, sin_signed):
        swapped = jnp.where(first_half, pltpu.roll(x, HEAD_PAD - half, 1),
                            pltpu.roll(x, half, 1))
        return x * cos + swapped * sin_signed

    u = h_ref[0].astype(BF16)
    cq = (_rms(_dot(u, wqd_ref[...])) * qg_ref[...]).astype(BF16)
    q_all = _dot(cq, wq_ref[...])
    k_pe = rope(_dot(u, wpe_ref[...]), cosk_ref[...], sink_ref[...])
    ckv = (_rms(_dot(u, wkvd_ref[...])) * kvg_ref[...]).astype(BF16)
    k_all = _dot(ckv, wk_ref[...])
    v_all = _dot(ckv, wv_ref[...])
    ones_col = jnp.where(lane == V_HEAD, 1.0, 0.0)
    cosq = cosq_ref[...]
    sinq = sinq_ref[...]
    for hd in range(MLA_HEADS):
        hl = slice(hd * HEAD_PAD, (hd + 1) * HEAD_PAD)
        q_ref[0, hd] = rope(q_all[:, hl], cosq, sinq).astype(BF16)
    for hd in range(MLA_HEADS):
        hl = slice(hd * HEAD_PAD, (hd + 1) * HEAD_PAD)
        k_ref[0, hd] = (k_all[:, hl] + k_pe).astype(BF16)
        v_ref[0, hd] = (v_all[:, hl] + ones_col).astype(BF16)


def _mla_proj(h, wkvd, kvg, wk, wv, wpe, cosk, sink, wqd, qg, wq, cosq, sinq):
    bsz, lp, _ = h.shape
    nt = lp // ROW_TILE
    row = pl.BlockSpec((1, ROW_TILE, D_MODEL), lambda b, t: (b, t, 0))
    tab = pl.BlockSpec((ROW_TILE, HEAD_PAD), lambda b, t: (t, 0))
    head = pl.BlockSpec((1, MLA_HEADS, ROW_TILE, HEAD_PAD), lambda b, t: (b, 0, t, 0))
    shape = jax.ShapeDtypeStruct((bsz, MLA_HEADS, lp, HEAD_PAD), BF16)
    c = _const_spec
    return pl.pallas_call(
        _mla_proj_kernel,
        grid=(bsz, nt),
        in_specs=[row, c(wkvd.shape), c(kvg.shape), c(wk.shape), c(wv.shape), c(wpe.shape),
                  tab, tab, c(wqd.shape), c(qg.shape), c(wq.shape), tab, tab],
        out_specs=[head, head, head],
        out_shape=[shape, shape, shape],
        compiler_params=pltpu.CompilerParams(
            dimension_semantics=("arbitrary", "arbitrary"), vmem_limit_bytes=VMEM_LIMIT),
        name="mla_proj",
    )(h, wkvd, kvg, wk, wv, wpe, cosk, sink, wqd, qg, wq, cosq, sinq)


def _attention_kernel(q_ref, k_ref, v_ref, o_ref, m_ref, acc_ref):
    i = pl.program_id(2)
    rows = ROW_TILE
    blk = LANES
    n_blk = rows // blk
    shift = CHUNK.bit_length() - 1
    col = lax.broadcasted_iota(jnp.int32, (rows, blk), 1)
    b_row = lax.broadcasted_iota(jnp.int32, (blk, blk), 0)
    b_col = lax.broadcasted_iota(jnp.int32, (blk, blk), 1)
    causal_blk = (b_col >> shift) <= (b_row >> shift)
    qs = [q_ref[0, hh] for hh in range(ATT_HEADS)]

    def real_keys(j, diag):
        return (b_col if diag else col) + j * rows >= PAD

    def scores(j, hh, valid0, diag):
        k0 = pl.multiple_of(j * rows, rows)
        s = lax.dot_general(qs[hh], k_ref[0, hh, pl.ds(k0, rows), :],
                            (((1,), (1,)), ((), ())), preferred_element_type=F32)
        if not diag:
            return jnp.concatenate([jnp.where(valid0, s[:, :blk], NEG_BIG), s[:, blk:]], axis=1)
        out = []
        for rb in range(n_blk):
            r_s = slice(rb * blk, (rb + 1) * blk)
            row = []
            for cb in range(n_blk):
                if cb > rb:
                    row.append(jnp.full((blk, blk), NEG_BIG, F32))
                    continue
                sb = s[r_s, cb * blk:(cb + 1) * blk]
                ok = causal_blk if cb == rb else None
                if cb == 0:
                    ok = valid0 if ok is None else ok & valid0
                row.append(sb if ok is None else jnp.where(ok, sb, NEG_BIG))
            out.append(jnp.concatenate(row, axis=1))
        return jnp.concatenate(out, axis=0)

    def absorb(j, hh, s):
        k0 = pl.multiple_of(j * rows, rows)
        m = m_ref[hh]
        m_new = jnp.maximum(m, jnp.max(s, axis=-1, keepdims=True))
        p = jnp.exp2(s - jnp.concatenate([m_new] * n_blk, axis=1)).astype(BF16)
        m_ref[hh] = m_new
        acc_ref[hh] = (jnp.exp2(m - m_new) * acc_ref[hh]
                       + _dot(p, v_ref[0, hh, pl.ds(k0, rows), :]))

    def steps(tiles, diag=False):
        valid = [real_keys(j, diag) for j in tiles]
        cur = [scores(tiles[0], hh, valid[0], diag) for hh in range(ATT_HEADS)]
        for n, j in enumerate(tiles):
            nxt = []
            for hh in range(ATT_HEADS):
                if n + 1 < len(tiles):
                    nxt.append(scores(tiles[n + 1], hh, valid[n + 1], diag))
                absorb(j, hh, cur[hh])
            cur = nxt

    m_ref[...] = jnp.full(m_ref.shape, NEG_BIG, F32)
    acc_ref[...] = jnp.zeros(acc_ref.shape, F32)
    start = 0
    for width, trips in ((1, i & 1), (2, (i >> 1) & 1), (4, i >> 2)):
        def group(g, _, start=start, width=width):
            steps([start + width * g + w for w in range(width)])
            return 0
        lax.fori_loop(0, trips, group, 0)
        start = start + width * trips
    steps([i], diag=True)
    outs = []
    for hh in range(ATT_HEADS):
        acc = acc_ref[hh]
        outs.append(acc[:, :V_HEAD] / acc[:, V_HEAD:V_HEAD + 1])
    o_ref[0] = jnp.concatenate(outs, axis=1).astype(BF16)


def _attention(q, k, v):
    bsz, nh, lp, _ = q.shape
    nt = lp // ROW_TILE
    hs = ATT_HEADS
    return pl.pallas_call(
        _attention_kernel,
        grid=(bsz, nh // hs, nt),
        in_specs=[pl.BlockSpec((1, hs, ROW_TILE, HEAD_PAD), lambda b, hp, i: (b, hp, i, 0)),
                  pl.BlockSpec((1, hs, lp, HEAD_PAD), lambda b, hp, i: (b, hp, 0, 0)),
                  pl.BlockSpec((1, hs, lp, HEAD_PAD), lambda b, hp, i: (b, hp, 0, 0))],
        out_specs=pl.BlockSpec((1, ROW_TILE, hs * V_HEAD), lambda b, hp, i: (b, i, hp)),
        out_shape=jax.ShapeDtypeStruct((bsz, lp, MLA_HEADS * V_HEAD), BF16),
        scratch_shapes=[pltpu.VMEM((hs, ROW_TILE, LANES), F32),
                        pltpu.VMEM((hs, ROW_TILE, HEAD_PAD), F32)],
        compiler_params=pltpu.CompilerParams(
            dimension_semantics=("arbitrary", "arbitrary", "arbitrary"),
            vmem_limit_bytes=VMEM_LIMIT),
        name="mla_attention",
    )(q, k, v)


def _row(v, width=None):
    v = v.astype(F32).reshape(1, -1)
    if width is not None and v.shape[1] < width:
        v = jnp.pad(v, ((0, 0), (0, width - v.shape[1])))
    return v


def _ffn_params(w_in, conv_w, conv_b, w_out):
    return (w_in.astype(BF16), conv_w.astype(F32), _row(conv_b),
            w_out.astype(BF16).reshape(N_FF_CHUNKS, FF_CHUNK, D_MODEL))


def _rope_tables(lp):
    half = QK_ROPE // 2
    pos = jnp.maximum(jnp.arange(lp, dtype=F32) - PAD, 0.0)
    inv_freq = ROPE_THETA ** (-jnp.arange(half, dtype=F32) * 2.0 / QK_ROPE)
    ang = pos[:, None] * inv_freq[None, :]
    cos, sin = jnp.cos(ang), jnp.sin(ang)
    zeros = lambda n: jnp.zeros((lp, n), F32)
    tail = HEAD_PAD - QK_NOPE - QK_ROPE
    cosk = jnp.concatenate([zeros(QK_NOPE), cos, cos, zeros(tail)], axis=1)
    sink = jnp.concatenate([zeros(QK_NOPE), -sin, sin, zeros(tail)], axis=1)
    c = (QK_NOPE + QK_ROPE) ** -0.5 * LOG2_E
    cosq = c * jnp.concatenate([jnp.ones((lp, QK_NOPE), F32), cos, cos, zeros(tail)], axis=1)
    return cosk, sink, cosq, c * sink


def _pad_last(a, before, total):
    pads = [(0, 0)] * (a.ndim - 1) + [(before, total - before - a.shape[-1])]
    return jnp.pad(a, pads)


def kernel(x, meta_tokens, m_w_in, m_conv_w, m_conv_b, m_dt_bias, m_a_log, m_d, m_norm_g, m_w_out, kv_w_down, kv_norm_g, kv_w_up, q_w_down, q_norm_g, q_w_up, o_w, f_w_in, f_conv_w, f_conv_b, f_w_out, ln_g, ln_b):
    bsz, seq, _ = x.shape
    lp = PAD + N_META + seq
    assert lp % ROW_TILE == 0 and PREFIX % CHUNK == 0 and ROW_TILE % PREFIX == 0
    assert seq % PREFIX == 0
    prefix = jnp.concatenate([jnp.zeros((PAD, D_MODEL), x.dtype), meta_tokens.astype(x.dtype)],
                             axis=0)

    w_in = m_w_in[0]
    wz = w_in[:, :SSM_D_INNER].astype(BF16)
    wxbc = w_in[:, SSM_D_INNER:SSM_D_INNER + SSM_CONV_DIM].astype(BF16)
    copies = HEAD_LANES // SSM_HEADS
    wdt = jnp.tile(w_in[:, SSM_D_INNER + SSM_CONV_DIM:], (1, copies)).astype(BF16)
    head_of_col = jnp.arange(SSM_D_INNER) // SSM_HEAD_DIM
    expand = (jnp.arange(HEAD_LANES)[:, None] % SSM_HEADS == head_of_col[None, :]).astype(BF16)
    pos = jnp.arange(ROW_TILE)
    tri = ((pos[:, None] // CHUNK == pos[None, :] // CHUNK)
           & (pos[None, :] <= pos[:, None])).astype(BF16)
    z, xbc, dt, da = _mamba_front(x, prefix, wz, wxbc, wdt, m_conv_w[0].astype(F32),
                                  _row(m_conv_b[0]), _row(jnp.tile(m_dt_bias[0], copies)),
                                  _row(jnp.tile(m_a_log[0], copies)))
    h = _mamba_scan(x, prefix, z, xbc, dt, da, tri, expand,
                    _row(jnp.repeat(m_d[0], SSM_HEAD_DIM)),
                    _row(m_norm_g[0]),
                    m_w_out[0].astype(BF16).reshape(SSM_GROUPS, GROUP_COLS, D_MODEL),
                    _row(ln_g[0, 0]), _row(ln_b[0, 0]))
    h = _ffn(h, *_ffn_params(f_w_in[0], f_conv_w[0], f_conv_b[0], f_w_out[0]),
             _row(ln_g[0, 1]), _row(ln_b[0, 1]))

    cosk, sink, cosq, sinq = _rope_tables(lp)
    w_rope = kv_w_down[:, KV_LORA:]
    wpe = _pad_last(w_rope, QK_NOPE, HEAD_PAD).astype(BF16)
    kv_up = kv_w_up.reshape(KV_LORA, MLA_HEADS, QK_NOPE + V_HEAD)
    wk = _pad_last(kv_up[..., :QK_NOPE], 0, HEAD_PAD).reshape(KV_LORA, -1).astype(BF16)
    wv = _pad_last(kv_up[..., QK_NOPE:], 0, HEAD_PAD).reshape(KV_LORA, -1).astype(BF16)
    q_up = q_w_up[0].reshape(Q_LORA, MLA_HEADS, QK_NOPE + QK_ROPE)
    wq = _pad_last(q_up, 0, HEAD_PAD).reshape(Q_LORA, -1).astype(BF16)
    q, k, v = _mla_proj(h, kv_w_down[:, :KV_LORA].astype(BF16), _row(kv_norm_g), wk, wv, wpe,
                        cosk, sink, q_w_down[0].astype(BF16), _row(q_norm_g[0]), wq, cosq, sinq)
    o = _attention(q, k, v)
    return _ffn(h, *_ffn_params(f_w_in[1], f_conv_w[1], f_conv_b[1], f_w_out[1]),
                _row(ln_g[1, 1]), _row(ln_b[1, 1]),
                attn=(o, o_w[0].astype(BF16), _row(ln_g[1, 0]), _row(ln_b[1, 0])))
```

```python
import math

import jax
import jax.numpy as jnp
from jax import lax
from jax.experimental import pallas as pl
from jax.experimental.pallas import tpu as pltpu

F32 = jnp.float32
BF16 = jnp.bfloat16

D_MODEL = 1024
N_META = 16
CHUNK = 64
DEPTH = 2
DN_ALPHA = (2.0 * DEPTH) ** 0.25
LN_EPS = 1e-5
RMS_EPS = 1e-6
LOG2_E = math.log2(math.e)

SSM_D_INNER = 2 * D_MODEL
SSM_HEAD_DIM = 64
SSM_HEADS = SSM_D_INNER // SSM_HEAD_DIM
SSM_GROUPS = 4
SSM_STATE = 128
SSM_CONV = 4
SSM_BC = SSM_GROUPS * SSM_STATE
SSM_CONV_DIM = SSM_D_INNER + 2 * SSM_BC
GROUP_COLS = SSM_D_INNER // SSM_GROUPS
HEAD_LANES = 128
FRONT_CHUNK = 512
FRONT_BUFFERS = 3

MLA_HEADS = 16
Q_LORA = 384
KV_LORA = 256
QK_NOPE = 64
QK_ROPE = 32
V_HEAD = 64
ROPE_THETA = 10000.0
HEAD_PAD = 128
ATT_HEADS = 8

D_FF = 2816
FFN_CONV = 3
FF_CHUNK = 256
N_FF_CHUNKS = D_FF // FF_CHUNK
FF_AHEAD = 3
SUBLANES = 8
LANES = 128

PAD = 112
PREFIX = PAD + N_META
ROW_TILE = 384
HALO = 8
NEG_BIG = -1e30
VMEM_LIMIT = 60 * 1024 * 1024


def _const_spec(shape):
    nd = len(shape)
    return pl.BlockSpec(shape, lambda *_: (0,) * nd, pipeline_mode=pl.Buffered(1))


def _layer_norm(v, g, b):
    mu = jnp.mean(v, axis=-1, keepdims=True)
    d = v - mu
    var = jnp.mean(d * d, axis=-1, keepdims=True)
    return d * lax.rsqrt(var + LN_EPS) * g + b


def _rms(v):
    return v * lax.rsqrt(jnp.mean(v * v, axis=-1, keepdims=True) + RMS_EPS)


def _silu(v):
    return v / (1.0 + jnp.exp(-v))


def _split_bf16(v, parts):
    out = []
    rem = v
    for _ in range(parts):
        p = rem.astype(BF16)
        out.append(p)
        rem = rem - p.astype(F32)
    return out


def _dot(a, b):
    return jnp.dot(a, b, preferred_element_type=F32)


def _packed_terms(v, parts):
    lane_copy = lax.broadcasted_iota(jnp.int32, v.shape, 1) // SSM_HEADS
    out = jnp.zeros(v.shape, BF16)
    for k, term in enumerate(_split_bf16(v, parts)):
        out = jnp.where(lane_copy == k, term, out)
    return out


def _sublane_major(perm_ref, v):
    rows, width = v.shape
    stride = rows // SUBLANES
    n = width // LANES
    for s in range(n):
        perm_ref[s] = v[:, s * LANES:(s + 1) * LANES]
    return jnp.concatenate(
        [jnp.concatenate([perm_ref[s, pl.ds(a, SUBLANES, stride=stride), :]
                          for s in range(n)], axis=1) for a in range(stride)], axis=0)


def _natural_order(perm_ref, v):
    rows, width = v.shape
    stride = rows // SUBLANES
    n = width // LANES
    for a in range(stride):
        for s in range(n):
            perm_ref[s, pl.ds(a, SUBLANES, stride=stride), :] = (
                v[a * SUBLANES:(a + 1) * SUBLANES, s * LANES:(s + 1) * LANES])
    return jnp.concatenate([perm_ref[s] for s in range(n)], axis=1)


def _matmul_rows(h, t):
    rows, width = h.shape
    p_i = lax.broadcasted_iota(jnp.int32, (rows, LANES), 0)
    time = (p_i % SUBLANES) * (rows // SUBLANES) + p_i // SUBLANES
    keep = time >= jnp.where(t > 0, 0, PAD)
    return jnp.concatenate([jnp.where(keep, h[:, s * LANES:(s + 1) * LANES], 0.0)
                            for s in range(width // LANES)], axis=1).astype(BF16)


def _causal_conv(x, halo_ref, cw_ref, cb_ref, cs):
    rows, width = x.shape
    taps = cw_ref.shape[0]
    first_sublane = lax.broadcasted_iota(jnp.int32, (SUBLANES, width), 0) == 0
    tail = halo_ref[:, cs]
    halo_ref[:, cs] = x[rows - (taps - 1) * SUBLANES:, :]
    wrapped = []
    for j in range(1, taps):
        cur = x[rows - j * SUBLANES:rows - (j - 1) * SUBLANES, :]
        prev = tail[(taps - 1 - j) * SUBLANES:(taps - j) * SUBLANES, :]
        wrapped.append(jnp.where(first_sublane, pltpu.roll(prev, 1, 0), pltpu.roll(cur, 1, 0)))
    out = cb_ref[:, cs] + cw_ref[taps - 1:taps, cs] * x
    for j in range(1, taps):
        shifted = jnp.concatenate(wrapped[j - 1::-1] + [x[:rows - j * SUBLANES, :]], axis=0)
        out = out + cw_ref[taps - 1 - j:taps - j, cs] * shifted
    return out


def _input_rows(x_refs, prefix_ref, t):
    first = jnp.where(t > 0, x_refs[0][0], prefix_ref[...])
    return jnp.concatenate([first] + [r[0] for r in x_refs[1:]], axis=0)


def _input_specs():
    n = ROW_TILE // PREFIX
    return [pl.BlockSpec((1, PREFIX, D_MODEL),
                         lambda b, t, k=k: (b, jnp.maximum(n * t - 1 + k, 0), 0))
            for k in range(n)] + [_const_spec((PREFIX, D_MODEL))]


def _mamba_front_kernel(xa_ref, xb_ref, xc_ref, prefix_ref,
                        wz_ref, wxbc_ref, wdt_ref, cw_ref, cb_ref, dtb_ref, alog_ref,
                        z_ref, xbc_ref, dt_ref, da_ref, halo_ref, buf_ref):
    t = pl.program_id(1)
    rows = ROW_TILE
    width = FRONT_CHUNK
    n_x = SSM_CONV_DIM // width
    n_z = SSM_D_INNER // width

    @pl.when(t == 0)
    def _():
        halo_ref[...] = jnp.zeros(halo_ref.shape, F32)

    u = _input_rows((xa_ref, xb_ref, xc_ref), prefix_ref, t).astype(BF16)

    def project(c):
        cs = slice(c * width, (c + 1) * width)
        slot = c % FRONT_BUFFERS
        buf_ref[slot, 0:HALO, :] = halo_ref[:, cs]
        pre = _dot(u, wxbc_ref[:, cs])
        buf_ref[slot, HALO:HALO + rows, :] = pre
        buf_ref[slot, HALO:HALO + PAD, :] = jnp.where(t > 0, pre[:PAD], 0.0)

    def gate(c):
        cs = slice(c * width, (c + 1) * width)
        z_ref[0, :, cs] = _dot(u, wz_ref[:, cs]).astype(BF16)

    for c in range(FRONT_BUFFERS - 1):
        project(c)
    for c in range(n_x):
        if c + FRONT_BUFFERS - 1 < n_x:
            project(c + FRONT_BUFFERS - 1)
        if c < n_z:
            gate(c)
        cs = slice(c * width, (c + 1) * width)
        slot = c % FRONT_BUFFERS
        acc = cb_ref[:, cs]
        for k in range(SSM_CONV):
            off = HALO - (SSM_CONV - 1) + k
            acc = acc + cw_ref[k:k + 1, cs] * buf_ref[slot, off:off + rows, :]
        halo_ref[:, cs] = buf_ref[slot, rows:rows + HALO, :]
        act = _silu(acc).astype(BF16)
        xbc_ref[0, :, cs] = act
        xbc_ref[0, 0:PAD, cs] = jnp.where(t > 0, act[:PAD], jnp.zeros((), BF16))

    x = _dot(u, wdt_ref[...]) + dtb_ref[...]
    dt = jnp.maximum(x, 0.0) + jnp.log1p(jnp.exp(-jnp.abs(x)))
    da = dt * (-jnp.exp(alog_ref[...]) * LOG2_E)
    dt_ref[0] = dt
    da_ref[0] = da
    dt_ref[0, 0:PAD, :] = jnp.where(t > 0, dt[:PAD], 0.0)
    da_ref[0, 0:PAD, :] = jnp.where(t > 0, da[:PAD], 0.0)


def _mamba_front(x, prefix, wz, wxbc, wdt, cw, cb, dtb, alog):
    bsz, seq, _ = x.shape
    lp = PREFIX + seq
    nt = lp // ROW_TILE
    row = lambda w: pl.BlockSpec((1, ROW_TILE, w), lambda b, t: (b, t, 0))
    return pl.pallas_call(
        _mamba_front_kernel,
        grid=(bsz, nt),
        in_specs=_input_specs() + [_const_spec(wz.shape), _const_spec(wxbc.shape),
                  _const_spec(wdt.shape), _const_spec(cw.shape), _const_spec(cb.shape),
                  _const_spec(dtb.shape), _const_spec(alog.shape)],
        out_specs=[row(SSM_D_INNER), row(SSM_CONV_DIM), row(HEAD_LANES), row(HEAD_LANES)],
        out_shape=[jax.ShapeDtypeStruct((bsz, lp, SSM_D_INNER), BF16),
                   jax.ShapeDtypeStruct((bsz, lp, SSM_CONV_DIM), BF16),
                   jax.ShapeDtypeStruct((bsz, lp, HEAD_LANES), F32),
                   jax.ShapeDtypeStruct((bsz, lp, HEAD_LANES), F32)],
        scratch_shapes=[pltpu.VMEM((HALO, SSM_CONV_DIM), F32),
                        pltpu.VMEM((FRONT_BUFFERS, ROW_TILE + HALO, FRONT_CHUNK), F32)],
        compiler_params=pltpu.CompilerParams(
            dimension_semantics=("arbitrary", "arbitrary"), vmem_limit_bytes=VMEM_LIMIT),
        name="mamba_front",
    )(x, x, x, prefix, wz, wxbc, wdt, cw, cb, dtb, alog)


def _mamba_scan_kernel(xa_ref, xb_ref, xc_ref, prefix_ref,
                       z_ref, xs_ref, b_ref, c_ref, dt_ref, da_ref, tri_ref, e_ref, dskip_ref,
                       ng_ref, wout_ref, lng_ref, lnb_ref, out_ref,
                       acs_ref, dte_ref, y_ref, state_ref):
    t = pl.program_id(1)
    rows = ROW_TILE
    n_chunks = rows // CHUNK

    @pl.when(t == 0)
    def _():
        state_ref[...] = jnp.zeros(state_ref.shape, F32)

    acs = sum(_dot(tri_ref[...], p) for p in _split_bf16(da_ref[0], 3))
    acs_ref[...] = _dot(_packed_terms(acs, 3), e_ref[...])
    dte_ref[...] = _dot(_packed_terms(dt_ref[0], 2), e_ref[...])

    s_i = lax.broadcasted_iota(jnp.int32, (CHUNK, 2 * CHUNK), 0)
    l_i = lax.broadcasted_iota(jnp.int32, (CHUNK, 2 * CHUNK), 1)
    diag = (l_i & (CHUNK - 1)) == s_i
    causal = (l_i & (CHUNK - 1)) <= s_i
    br_i = lax.broadcasted_iota(jnp.int32, (2 * CHUNK, 2 * CHUNK), 0)
    bc_i = lax.broadcasted_iota(jnp.int32, (2 * CHUNK, 2 * CHUNK), 1)
    blockdiag = (br_i // CHUNK) == (bc_i // CHUNK)

    def chunk_body(c, carry):
        r0 = pl.multiple_of(c * CHUNK, CHUNK)
        acs_c = acs_ref[pl.ds(r0, CHUNK), :]
        dte_c = dte_ref[pl.ds(r0, CHUNK), :]
        xs_c = xs_ref[0, pl.ds(r0, CHUNK), :]
        b_c = b_ref[0, pl.ds(r0, CHUNK), :]
        c_c = c_ref[0, pl.ds(r0, CHUNK), :]
        last = acs_c[CHUNK - 1:CHUNK, :]
        groups = [slice(g * SSM_STATE, (g + 1) * SSM_STATE) for g in range(SSM_GROUPS)]
        gcols = [slice(g * GROUP_COLS, (g + 1) * GROUP_COLS) for g in range(SSM_GROUPS)]
        pair = 2 * SSM_HEAD_DIM
        states = [state_ref[g] for g in range(SSM_GROUPS)]
        cb2 = [lax.dot_general(c_c[:, gl], jnp.concatenate([b_c[:, gl], b_c[:, gl]], axis=0),
                               (((1,), (1,)), ((), ())), preferred_element_type=F32)
               for gl in groups]
        y_off = [_dot(c_c[:, gl], st.astype(BF16)) for gl, st in zip(groups, states)]
        x_dt = xs_c.astype(F32) * dte_c
        xw = (x_dt * jnp.exp2(last - acs_c)).astype(BF16)
        x_dt = x_dt.astype(BF16)
        s_new = [lax.dot_general(b_c[:, gl], xw[:, gc], (((0,), (0,)), ((), ())),
                                 preferred_element_type=F32) for gl, gc in zip(groups, gcols)]
        grow = jnp.exp2(acs_c)
        chunk_decay = jnp.exp2(last)
        for g in range(SSM_GROUPS):
            for jj in range(GROUP_COLS // pair):
                pl_ = slice(g * GROUP_COLS + jj * pair, g * GROUP_COLS + (jj + 1) * pair)
                a_p = acs_c[:, pl_]
                a_row = jnp.sum(jnp.where(diag, a_p, 0.0), axis=0, keepdims=True)
                decay = jnp.exp2(jnp.where(causal, a_p - a_row, NEG_BIG))
                m2 = (cb2[g] * decay).astype(BF16)
                x_p = x_dt[:, pl_]
                x_bd = jnp.where(blockdiag, jnp.concatenate([x_p, x_p], axis=0),
                                 jnp.zeros((), BF16))
                y_ref[pl.ds(r0, CHUNK), pl_] = (
                    _dot(m2, x_bd) + y_off[g][:, jj * pair:(jj + 1) * pair] * grow[:, pl_])
            state_ref[g] = states[g] * chunk_decay[:, gcols[g]] + s_new[g]
        return carry

    lax.fori_loop(0, n_chunks, chunk_body, 0, unroll=True)

    mix = jnp.zeros((rows, D_MODEL), F32)
    for g in range(SSM_GROUPS):
        gc = slice(g * GROUP_COLS, (g + 1) * GROUP_COLS)
        y = y_ref[:, gc] + xs_ref[0, :, gc].astype(F32) * dskip_ref[:, gc]
        yn = _rms(y * _silu(z_ref[0, :, gc].astype(F32))) * ng_ref[:, gc]
        mix = mix + _dot(yn.astype(BF16), wout_ref[g])
    h = _input_rows((xa_ref, xb_ref, xc_ref), prefix_ref, t)
    out_ref[0] = _layer_norm(DN_ALPHA * h + mix, lng_ref[...], lnb_ref[...])


def _mamba_scan(x, prefix, z, xbc, dt, da, tri, e, dskip, ng, wout, lng, lnb):
    bsz, seq, _ = x.shape
    lp = PREFIX + seq
    nt = lp // ROW_TILE
    row = lambda w, cb=0: pl.BlockSpec((1, ROW_TILE, w), lambda b, t: (b, t, cb))
    return pl.pallas_call(
        _mamba_scan_kernel,
        grid=(bsz, nt),
        in_specs=_input_specs() + [row(SSM_D_INNER), row(SSM_D_INNER),
                  row(SSM_BC, SSM_D_INNER // SSM_BC), row(SSM_BC, SSM_D_INNER // SSM_BC + 1),
                  row(HEAD_LANES), row(HEAD_LANES), _const_spec(tri.shape),
                  _const_spec(e.shape), _const_spec(dskip.shape), _const_spec(ng.shape),
                  _const_spec(wout.shape), _const_spec(lng.shape), _const_spec(lnb.shape)],
        out_specs=row(D_MODEL),
        out_shape=jax.ShapeDtypeStruct((bsz, lp, D_MODEL), F32),
        scratch_shapes=[pltpu.VMEM((ROW_TILE, SSM_D_INNER), F32),
                        pltpu.VMEM((ROW_TILE, SSM_D_INNER), F32),
                        pltpu.VMEM((ROW_TILE, SSM_D_INNER), F32),
                        pltpu.VMEM((SSM_GROUPS, SSM_STATE, GROUP_COLS), F32)],
        compiler_params=pltpu.CompilerParams(
            dimension_semantics=("arbitrary", "arbitrary"), vmem_limit_bytes=VMEM_LIMIT),
        name="mamba_scan",
    )(x, x, x, prefix, z, xbc, xbc, xbc, dt, da, tri, e, dskip, ng, wout, lng, lnb)


def _ffn_body(h, t, win_ref, cw_ref, cb_ref, wout_ref, lng_ref, lnb_ref, halo_ref, perm_ref):
    rows = ROW_TILE

    @pl.when(t == 0)
    def _():
        halo_ref[...] = jnp.zeros(halo_ref.shape, F32)

    h = _sublane_major(perm_ref, h)
    u = _matmul_rows(h, t)

    def cols(cb):
        return (slice(cb * FF_CHUNK, (cb + 1) * FF_CHUNK),
                slice(D_FF + cb * FF_CHUNK, D_FF + (cb + 1) * FF_CHUNK))

    def conv(x, cs):
        return _causal_conv(x, halo_ref, cw_ref, cb_ref, cs)

    def project(cb):
        return [_dot(u, win_ref[:, cs]) for cs in cols(cb)]

    pre = {cb: project(cb) for cb in range(FF_AHEAD)}
    acc = jnp.zeros((rows, D_MODEL), F32)
    for cb in range(N_FF_CHUNKS):
        if cb + FF_AHEAD < N_FF_CHUNKS:
            pre[cb + FF_AHEAD] = project(cb + FF_AHEAD)
        gate, up = pre.pop(cb)
        gate_cols, up_cols = cols(cb)
        act = (_silu(conv(gate, gate_cols)) * conv(up, up_cols)).astype(BF16)
        acc = acc + _dot(act, wout_ref[cb])
    out = _layer_norm(DN_ALPHA * h + acc, lng_ref[...], lnb_ref[...])
    return _natural_order(perm_ref, out)


def _ffn_kernel(h_ref, win_ref, cw_ref, cb_ref, wout_ref, lng_ref, lnb_ref, out_ref,
                halo_ref, perm_ref):
    out_ref[0] = _ffn_body(h_ref[0], pl.program_id(1), win_ref, cw_ref, cb_ref, wout_ref,
                           lng_ref, lnb_ref, halo_ref, perm_ref)


def _attn_out_ffn_kernel(h_ref, o_ref, wo_ref, lng0_ref, lnb0_ref, win_ref, cw_ref, cb_ref,
                         wout_ref, lng_ref, lnb_ref, out_hbm, halo_ref, perm_ref, stage_ref,
                         sem):
    b, t = pl.program_id(0), pl.program_id(1)
    nt = pl.num_programs(1)
    rows = ROW_TILE
    mix = _dot(o_ref[0], wo_ref[...])
    h = _layer_norm(DN_ALPHA * h_ref[0] + mix, lng0_ref[...], lnb0_ref[...])
    step = b * nt + t
    slot = step % 2
    stage_ref[slot] = _ffn_body(h, t, win_ref, cw_ref, cb_ref, wout_ref, lng_ref, lnb_ref,
                                halo_ref, perm_ref)

    def first_tile_copy(slot_, b_):
        return pltpu.make_async_copy(stage_ref.at[slot_, PREFIX:rows],
                                     out_hbm.at[b_, 0:rows - PREFIX], sem.at[slot_])

    def tile_copy(slot_, b_, t_):
        start = pl.multiple_of(t_ * rows - PREFIX, PREFIX)
        return pltpu.make_async_copy(stage_ref.at[slot_],
                                     out_hbm.at[b_, pl.ds(start, rows)], sem.at[slot_])

    @pl.when(t == 1)
    def _():
        first_tile_copy(1 - slot, b).wait()

    @pl.when(t > 1)
    def _():
        tile_copy(1 - slot, b, t - 1).wait()

    @pl.when((t == 0) & (b > 0))
    def _():
        tile_copy(1 - slot, b - 1, nt - 1).wait()

    @pl.when(t == 0)
    def _():
        first_tile_copy(slot, b).start()

    @pl.when(t > 0)
    def _():
        tile_copy(slot, b, t).start()

    @pl.when(step == pl.num_programs(0) * nt - 1)
    def _():
        tile_copy(slot, b, t).wait()


def _ffn(h, win, cw, cb, wout, lng, lnb, attn=None):
    bsz, lp, _ = h.shape
    nt = lp // ROW_TILE
    row = pl.BlockSpec((1, ROW_TILE, D_MODEL), lambda b, t: (b, t, 0))
    consts = (win, cw, cb, wout, lng, lnb)
    scratch = [pltpu.VMEM((2 * SUBLANES, 2 * D_FF), F32),
               pltpu.VMEM((D_MODEL // LANES, ROW_TILE, LANES), F32)]
    if attn is None:
        kern, args, specs = _ffn_kernel, (h,) + consts, [row]
        out_spec = row
        out_shape = jax.ShapeDtypeStruct((bsz, lp, D_MODEL), F32)
    else:
        o, wo, lng0, lnb0 = attn
        kern = _attn_out_ffn_kernel
        args = (h, o, wo, lng0, lnb0) + consts
        specs = [row, row, _const_spec(wo.shape), _const_spec(lng0.shape),
                 _const_spec(lnb0.shape)]
        out_spec = pl.BlockSpec(memory_space=pl.ANY)
        out_shape = jax.ShapeDtypeStruct((bsz, lp - PREFIX, D_MODEL), F32)
        scratch += [pltpu.VMEM((2, ROW_TILE, D_MODEL), F32), pltpu.SemaphoreType.DMA((2,))]
    return pl.pallas_call(
        kern,
        grid=(bsz, nt),
        in_specs=specs + [_const_spec(a.shape) for a in consts],
        out_specs=out_spec,
        out_shape=out_shape,
        scratch_shapes=scratch,
        compiler_params=pltpu.CompilerParams(
            dimension_semantics=("arbitrary", "arbitrary"), vmem_limit_bytes=VMEM_LIMIT),
        name="conv_ffn" if attn is None else "attn_out_conv_ffn",
    )(*args)


def _mla_proj_kernel(h_ref, wkvd_ref, kvg_ref, wk_ref, wv_ref, wpe_ref, cosk_ref,
                     sink_ref, wqd_ref, qg_ref, wq_ref, cosq_ref, sinq_ref,
                     q_ref, k_ref, v_ref):
    half = QK_ROPE // 2
    lane = lax.broadcasted_iota(jnp.int32, (1, HEAD_PAD), 1)
    first_half = lane < QK_NOPE + half

    def rope(x, cos, sin_signed):
        swapped = jnp.where(first_half, pltpu.roll(x, HEAD_PAD - half, 1),
                            pltpu.roll(x, half, 1))
        return x * cos + swapped * sin_signed

    u = h_ref[0].astype(BF16)
    cq = (_rms(_dot(u, wqd_ref[...])) * qg_ref[...]).astype(BF16)
    q_all = _dot(cq, wq_ref[...])
    k_pe = rope(_dot(u, wpe_ref[...]), cosk_ref[...], sink_ref[...])
    ckv = (_rms(_dot(u, wkvd_ref[...])) * kvg_ref[...]).astype(BF16)
    k_all = _dot(ckv, wk_ref[...])
    v_all = _dot(ckv, wv_ref[...])
    ones_col = jnp.where(lane == V_HEAD, 1.0, 0.0)
    cosq = cosq_ref[...]
    sinq = sinq_ref[...]
    for hd in range(MLA_HEADS):
        hl = slice(hd * HEAD_PAD, (hd + 1) * HEAD_PAD)
        q_ref[0, hd] = rope(q_all[:, hl], cosq, sinq).astype(BF16)
    for hd in range(MLA_HEADS):
        hl = slice(hd * HEAD_PAD, (hd + 1) * HEAD_PAD)
        k_ref[0, hd] = (k_all[:, hl] + k_pe).astype(BF16)
        v_ref[0, hd] = (v_all[:, hl] + ones_col).astype(BF16)


def _mla_proj(h, wkvd, kvg, wk, wv, wpe, cosk, sink, wqd, qg, wq, cosq, sinq):
    bsz, lp, _ = h.shape
    nt = lp // ROW_TILE
    row = pl.BlockSpec((1, ROW_TILE, D_MODEL), lambda b, t: (b, t, 0))
    tab = pl.BlockSpec((ROW_TILE, HEAD_PAD), lambda b, t: (t, 0))
    head = pl.BlockSpec((1, MLA_HEADS, ROW_TILE, HEAD_PAD), lambda b, t: (b, 0, t, 0))
    shape = jax.ShapeDtypeStruct((bsz, MLA_HEADS, lp, HEAD_PAD), BF16)
    c = _const_spec
    return pl.pallas_call(
        _mla_proj_kernel,
        grid=(bsz, nt),
        in_specs=[row, c(wkvd.shape), c(kvg.shape), c(wk.shape), c(wv.shape), c(wpe.shape),
                  tab, tab, c(wqd.shape), c(qg.shape), c(wq.shape), tab, tab],
        out_specs=[head, head, head],
        out_shape=[shape, shape, shape],
        compiler_params=pltpu.CompilerParams(
            dimension_semantics=("arbitrary", "arbitrary"), vmem_limit_bytes=VMEM_LIMIT),
        name="mla_proj",
    )(h, wkvd, kvg, wk, wv, wpe, cosk, sink, wqd, qg, wq, cosq, sinq)


def _attention_kernel(q_ref, k_ref, v_ref, o_ref, m_ref, acc_ref):
    i = pl.program_id(2)
    rows = ROW_TILE
    blk = LANES
    n_blk = rows // blk
    shift = CHUNK.bit_length() - 1
    col = lax.broadcasted_iota(jnp.int32, (rows, blk), 1)
    b_row = lax.broadcasted_iota(jnp.int32, (blk, blk), 0)
    b_col = lax.broadcasted_iota(jnp.int32, (blk, blk), 1)
    causal_blk = (b_col >> shift) <= (b_row >> shift)
    qs = [q_ref[0, hh] for hh in range(ATT_HEADS)]

    def real_keys(j, diag):
        return (b_col if diag else col) + j * rows >= PAD

    def scores(j, hh, valid0, diag):
        k0 = pl.multiple_of(j * rows, rows)
        s = lax.dot_general(qs[hh], k_ref[0, hh, pl.ds(k0, rows), :],
                            (((1,), (1,)), ((), ())), preferred_element_type=F32)
        if not diag:
            return jnp.concatenate([jnp.where(valid0, s[:, :blk], NEG_BIG), s[:, blk:]], axis=1)
        out = []
        for rb in range(n_blk):
            r_s = slice(rb * blk, (rb + 1) * blk)
            row = []
            for cb in range(n_blk):
                if cb > rb:
                    row.append(jnp.full((blk, blk), NEG_BIG, F32))
                    continue
                sb = s[r_s, cb * blk:(cb + 1) * blk]
                ok = causal_blk if cb == rb else None
                if cb == 0:
                    ok = valid0 if ok is None else ok & valid0
                row.append(sb if ok is None else jnp.where(ok, sb, NEG_BIG))
            out.append(jnp.concatenate(row, axis=1))
        return jnp.concatenate(out, axis=0)

    def absorb(j, hh, s):
        k0 = pl.multiple_of(j * rows, rows)
        m = m_ref[hh]
        m_new = jnp.maximum(m, jnp.max(s, axis=-1, keepdims=True))
        p = jnp.exp2(s - jnp.concatenate([m_new] * n_blk, axis=1)).astype(BF16)
        m_ref[hh] = m_new
        acc_ref[hh] = (jnp.exp2(m - m_new) * acc_ref[hh]
                       + _dot(p, v_ref[0, hh, pl.ds(k0, rows), :]))

    def steps(tiles, diag=False):
        valid = [real_keys(j, diag) for j in tiles]
        cur = [scores(tiles[0], hh, valid[0], diag) for hh in range(ATT_HEADS)]
        for n, j in enumerate(tiles):
            nxt = []
            for hh in range(ATT_HEADS):
                if n + 1 < len(tiles):
                    nxt.append(scores(tiles[n + 1], hh, valid[n + 1], diag))
                absorb(j, hh, cur[hh])
            cur = nxt

    m_ref[...] = jnp.full(m_ref.shape, NEG_BIG, F32)
    acc_ref[...] = jnp.zeros(acc_ref.shape, F32)
    start = 0
    for width, trips in ((1, i & 1), (2, (i >> 1) & 1), (4, i >> 2)):
        def group(g, _, start=start, width=width):
            steps([start + width * g + w for w in range(width)])
            return 0
        lax.fori_loop(0, trips, group, 0)
        start = start + width * trips
    steps([i], diag=True)
    outs = []
    for hh in range(ATT_HEADS):
        acc = acc_ref[hh]
        outs.append(acc[:, :V_HEAD] / acc[:, V_HEAD:V_HEAD + 1])
    o_ref[0] = jnp.concatenate(outs, axis=1).astype(BF16)


def _attention(q, k, v):
    bsz, nh, lp, _ = q.shape
    nt = lp // ROW_TILE
    hs = ATT_HEADS
    return pl.pallas_call(
        _attention_kernel,
        grid=(bsz, nh // hs, nt),
        in_specs=[pl.BlockSpec((1, hs, ROW_TILE, HEAD_PAD), lambda b, hp, i: (b, hp, i, 0)),
                  pl.BlockSpec((1, hs, lp, HEAD_PAD), lambda b, hp, i: (b, hp, 0, 0)),
                  pl.BlockSpec((1, hs, lp, HEAD_PAD), lambda b, hp, i: (b, hp, 0, 0))],
        out_specs=pl.BlockSpec((1, ROW_TILE, hs * V_HEAD), lambda b, hp, i: (b, i, hp)),
        out_shape=jax.ShapeDtypeStruct((bsz, lp, MLA_HEADS * V_HEAD), BF16),
        scratch_shapes=[pltpu.VMEM((hs, ROW_TILE, LANES), F32),
                        pltpu.VMEM((hs, ROW_TILE, HEAD_PAD), F32)],
        compiler_params=pltpu.CompilerParams(
            dimension_semantics=("arbitrary", "arbitrary", "arbitrary"),
            vmem_limit_bytes=VMEM_LIMIT),
        name="mla_attention",
    )(q, k, v)


def _row(v, width=None):
    v = v.astype(F32).reshape(1, -1)
    if width is not None and v.shape[1] < width:
        v = jnp.pad(v, ((0, 0), (0, width - v.shape[1])))
    return v


def _ffn_params(w_in, conv_w, conv_b, w_out):
    return (w_in.astype(BF16), conv_w.astype(F32), _row(conv_b),
            w_out.astype(BF16).reshape(N_FF_CHUNKS, FF_CHUNK, D_MODEL))


def _rope_tables(lp):
    half = QK_ROPE // 2
    pos = jnp.maximum(jnp.arange(lp, dtype=F32) - PAD, 0.0)
    inv_freq = ROPE_THETA ** (-jnp.arange(half, dtype=F32) * 2.0 / QK_ROPE)
    ang = pos[:, None] * inv_freq[None, :]
    cos, sin = jnp.cos(ang), jnp.sin(ang)
    zeros = lambda n: jnp.zeros((lp, n), F32)
    tail = HEAD_PAD - QK_NOPE - QK_ROPE
    cosk = jnp.concatenate([zeros(QK_NOPE), cos, cos, zeros(tail)], axis=1)
    sink = jnp.concatenate([zeros(QK_NOPE), -sin, sin, zeros(tail)], axis=1)
    c = (QK_NOPE + QK_ROPE) ** -0.5 * LOG2_E
    cosq = c * jnp.concatenate([jnp.ones((lp, QK_NOPE), F32), cos, cos, zeros(tail)], axis=1)
    return cosk, sink, cosq, c * sink


def _pad_last(a, before, total):
    pads = [(0, 0)] * (a.ndim - 1) + [(before, total - before - a.shape[-1])]
    return jnp.pad(a, pads)


def kernel(x, meta_tokens, m_w_in, m_conv_w, m_conv_b, m_dt_bias, m_a_log, m_d, m_norm_g, m_w_out, kv_w_down, kv_norm_g, kv_w_up, q_w_down, q_norm_g, q_w_up, o_w, f_w_in, f_conv_w, f_conv_b, f_w_out, ln_g, ln_b):
    bsz, seq, _ = x.shape
    lp = PAD + N_META + seq
    assert lp % ROW_TILE == 0 and PREFIX % CHUNK == 0 and ROW_TILE % PREFIX == 0
    assert seq % PREFIX == 0
    prefix = jnp.concatenate([jnp.zeros((PAD, D_MODEL), x.dtype), meta_tokens.astype(x.dtype)],
                             axis=0)

    w_in = m_w_in[0]
    wz = w_in[:, :SSM_D_INNER].astype(BF16)
    wxbc = w_in[:, SSM_D_INNER:SSM_D_INNER + SSM_CONV_DIM].astype(BF16)
    copies = HEAD_LANES // SSM_HEADS
    wdt = jnp.tile(w_in[:, SSM_D_INNER + SSM_CONV_DIM:], (1, copies)).astype(BF16)
    head_of_col = jnp.arange(SSM_D_INNER) // SSM_HEAD_DIM
    expand = (jnp.arange(HEAD_LANES)[:, None] % SSM_HEADS == head_of_col[None, :]).astype(BF16)
    pos = jnp.arange(ROW_TILE)
    tri = ((pos[:, None] // CHUNK == pos[None, :] // CHUNK)
           & (pos[None, :] <= pos[:, None])).astype(BF16)
    z, xbc, dt, da = _mamba_front(x, prefix, wz, wxbc, wdt, m_conv_w[0].astype(F32),
                                  _row(m_conv_b[0]), _row(jnp.tile(m_dt_bias[0], copies)),
                                  _row(jnp.tile(m_a_log[0], copies)))
    h = _mamba_scan(x, prefix, z, xbc, dt, da, tri, expand,
                    _row(jnp.repeat(m_d[0], SSM_HEAD_DIM)),
                    _row(m_norm_g[0]),
                    m_w_out[0].astype(BF16).reshape(SSM_GROUPS, GROUP_COLS, D_MODEL),
                    _row(ln_g[0, 0]), _row(ln_b[0, 0]))
    h = _ffn(h, *_ffn_params(f_w_in[0], f_conv_w[0], f_conv_b[0], f_w_out[0]),
             _row(ln_g[0, 1]), _row(ln_b[0, 1]))

    cosk, sink, cosq, sinq = _rope_tables(lp)
    w_rope = kv_w_down[:, KV_LORA:]
    wpe = _pad_last(w_rope, QK_NOPE, HEAD_PAD).astype(BF16)
    kv_up = kv_w_up.reshape(KV_LORA, MLA_HEADS, QK_NOPE + V_HEAD)
    wk = _pad_last(kv_up[..., :QK_NOPE], 0, HEAD_PAD).reshape(KV_LORA, -1).astype(BF16)
    wv = _pad_last(kv_up[..., QK_NOPE:], 0, HEAD_PAD).reshape(KV_LORA, -1).astype(BF16)
    q_up = q_w_up[0].reshape(Q_LORA, MLA_HEADS, QK_NOPE + QK_ROPE)
    wq = _pad_last(q_up, 0, HEAD_PAD).reshape(Q_LORA, -1).astype(BF16)
    q, k, v = _mla_proj(h, kv_w_down[:, :KV_LORA].astype(BF16), _row(kv_norm_g), wk, wv, wpe,
                        cosk, sink, q_w_down[0].astype(BF16), _row(q_norm_g[0]), wq, cosq, sinq)
    o = _attention(q, k, v)
    return _ffn(h, *_ffn_params(f_w_in[1], f_conv_w[1], f_conv_b[1], f_w_out[1]),
                _row(ln_g[1, 1]), _row(ln_b[1, 1]),
                attn=(o, o_w[0].astype(BF16), _row(ln_g[1, 0]), _row(ln_b[1, 0])))
```
